```python
import jax, jax.numpy as jnp
from jax import lax
import numpy as np

D_MODEL = 1024
BATCH = 8
SEQ = 2048
DEPTH = 4
DEC_BATCH = 128
DEC_SEQ = 1
PAST_LEN = 16384
PAGE_SIZE = 128

CHUNK = 128
A_GROUPS = 4
D_A = D_MODEL
A_GROUP_DIM = D_A // A_GROUPS
R_HEADS = 4
R_DK = D_MODEL // R_HEADS
R_DV = D_MODEL // R_HEADS
D_R = R_HEADS * R_DV
D_FF = 4 * D_MODEL
D_IN = 2 * D_A + 2 * R_HEADS * R_DK + 2 * D_R + 2 * D_MODEL
ROPE_BASE = 10000.0
EPS = 1e-6

kernel_name = "gated_gmlp_retention_hybrid_step"


def rms_norm(x, g):
    xf = x.astype(jnp.float32)
    y = xf * lax.rsqrt(jnp.mean(xf * xf, axis=-1, keepdims=True) + EPS)
    return (y * g.astype(jnp.float32)).astype(x.dtype)


def layer_norm(x, g, b):
    xf = x.astype(jnp.float32)
    mu = jnp.mean(xf, axis=-1, keepdims=True)
    var = jnp.mean(jnp.square(xf - mu), axis=-1, keepdims=True)
    return ((xf - mu) * lax.rsqrt(var + EPS) * g.astype(jnp.float32) + b.astype(jnp.float32)).astype(x.dtype)


def rotary(x, pos):
    half = x.shape[-1] // 2
    inv = ROPE_BASE ** (-jnp.arange(half, dtype=jnp.float32) / half)
    ang = pos[:, None] * inv[None, :]
    cos = jnp.cos(ang)[None, :, None, :]
    sin = jnp.sin(ang)[None, :, None, :]
    x1, x2 = x[..., :half], x[..., half:]
    return jnp.concatenate([x1 * cos - x2 * sin, x1 * sin + x2 * cos], axis=-1)


def chunk_mlp(u, v, w_s, b_s):
    B, T, _ = v.shape
    L = CHUNK if T % CHUNK == 0 else T
    n = T // L
    mask = jnp.tril(jnp.ones((L, L), dtype=bool))
    w = jnp.where(mask[None], w_s[:, :L, :L], jnp.zeros((), w_s.dtype)).astype(v.dtype)
    vc = v.reshape(B, n, L, A_GROUPS, A_GROUP_DIM)
    z = jnp.einsum('gts,bnsgd->bntgd', w, vc) + b_s[:, :L].T.astype(v.dtype)[None, None, :, :, None]
    return u * z.reshape(B, T, D_A)


def retention(q, k, v, state0):
    B, T = q.shape[:2]
    L = CHUNK if T % CHUNK == 0 else T
    n = T // L
    log_g = jnp.log1p(-jnp.exp2(-5.0 - jnp.arange(R_HEADS, dtype=jnp.float32)))
    idx = jnp.arange(L, dtype=jnp.float32)
    diff = idx[:, None] - idx[None, :]
    decay = jnp.where(diff[None] >= 0.0,
                      jnp.exp(jnp.maximum(diff, 0.0)[None] * log_g[:, None, None]), 0.0)
    xi = jnp.exp((idx[:, None] + 1.0) * log_g[None, :])
    zeta = jnp.exp((L - 1.0 - idx)[:, None] * log_g[None, :])
    g_L = jnp.exp(L * log_g)

    def split(t):
        return t.reshape(B, n, L, *t.shape[2:]).swapaxes(0, 1)

    def step(S, blk):
        qc, kc, vc = blk
        sc = jnp.einsum('blhk,bmhk->bhlm', qc, kc) * decay[None]
        o = (jnp.einsum('bhlm,bmhv->blhv', sc, vc)
             + jnp.einsum('blhk,bhkv->blhv', qc, S) * xi[None, :, :, None])
        S = S * g_L[None, :, None, None] + jnp.einsum('blhk,blhv->bhkv', kc * zeta[None, :, :, None], vc)
        return S, o

    S, o = lax.scan(step, state0, (split(q), split(k), split(v)))
    return o.swapaxes(0, 1).reshape(B, T, R_HEADS, R_DV), S


def trunk(x, c, pos0, ret_state0, w_ada, b_ada, norm1_g, w_in, ln_v_g, ln_v_b, w_s, b_s,
          gn_g, w_out, norm2_g, w_ff1, w_ff2, final_g):
    B, T, _ = x.shape
    pos = pos0 + jnp.arange(T, dtype=jnp.float32)
    sizes = [D_A, D_A, R_HEADS * R_DK, R_HEADS * R_DK, D_R, D_R, D_MODEL]
    offs = []
    s = 0
    for n_cols in sizes:
        s += n_cols
        offs.append(s)
    ret_states, v_rows = [], []
    for l in range(DEPTH):
        mod = (c @ w_ada[l] + b_ada[l])[:, None, :]
        sh1, sc1, gt1, sh2, sc2, gt2 = jnp.split(mod, 6, axis=-1)
        h = rms_norm(x, norm1_g[l]) * (1.0 + sc1) + sh1
        proj = h @ w_in[l]
        u, va, q, k, vr, g, ga, gb = jnp.split(proj, offs, axis=-1)
        va_n = layer_norm(va, ln_v_g[l], ln_v_b[l])
        a = chunk_mlp(u, va_n, w_s[l], b_s[l])
        v_rows.append(va_n)
        qh = rotary(q.reshape(B, T, R_HEADS, R_DK).astype(jnp.float32), pos)
        kh = rotary(k.reshape(B, T, R_HEADS, R_DK).astype(jnp.float32), pos) * (R_DK ** -0.5)
        vh = vr.reshape(B, T, R_HEADS, R_DV).astype(jnp.float32)
        if ret_state0 is None:
            S0 = jnp.zeros((B, R_HEADS, R_DK, R_DV), jnp.float32)
        else:
            S0 = ret_state0[l].astype(jnp.float32)
        o, S = retention(qh, kh, vh, S0)
        ret_states.append(S.astype(x.dtype))
        mu = jnp.mean(o, axis=-1, keepdims=True)
        var = jnp.mean(jnp.square(o - mu), axis=-1, keepdims=True)
        o_n = ((o - mu) * lax.rsqrt(var + EPS)).reshape(B, T, D_R) * gn_g[l].astype(jnp.float32)
        r = (jax.nn.silu(g.astype(jnp.float32)) * o_n).astype(x.dtype)
        m = jax.nn.sigmoid(ga) * a + jax.nn.sigmoid(gb) * r
        x = x + gt1 * (m @ w_out[l])
        h2 = rms_norm(x, norm2_g[l]) * (1.0 + sc2) + sh2
        x = x + gt2 * (jnp.square(jax.nn.relu(h2 @ w_ff1[l])) @ w_ff2[l])
    return rms_norm(x, final_g), jnp.stack(ret_states), jnp.stack(v_rows)


def setup_inputs(seed: int = 0) -> dict:
    key = jax.random.key(seed)
    ks = jax.random.split(key, 24)
    f32 = jnp.float32
    nrm = lambda k, shp, s: jax.random.normal(k, shp, f32) * s
    return {
        "x_prompt": nrm(ks[0], (BATCH, SEQ, D_MODEL), 1.0),
        "x_sample": nrm(ks[1], (DEC_BATCH, DEC_SEQ, D_MODEL), 1.0),
        "state_ret": nrm(ks[2], (DEPTH, DEC_BATCH, R_HEADS, R_DK, R_DV), 0.5),
        "c_prompt": nrm(ks[3], (BATCH, D_MODEL), 1.0),
        "c_sample": nrm(ks[4], (DEC_BATCH, D_MODEL), 1.0),
        "w_ada": nrm(ks[5], (DEPTH, D_MODEL, 6 * D_MODEL), 0.5 * D_MODEL ** -0.5),
        "b_ada": nrm(ks[6], (DEPTH, 6 * D_MODEL), 0.02),
        "norm1_g": 1.0 + nrm(ks[7], (DEPTH, D_MODEL), 0.02),
        "w_in": nrm(ks[8], (DEPTH, D_MODEL, D_IN), D_MODEL ** -0.5),
        "ln_v_g": 1.0 + nrm(ks[9], (DEPTH, D_A), 0.02),
        "ln_v_b": nrm(ks[10], (DEPTH, D_A), 0.02),
        "w_s": nrm(ks[11], (DEPTH, A_GROUPS, CHUNK, CHUNK), CHUNK ** -0.5),
        "b_s": 1.0 + nrm(ks[12], (DEPTH, A_GROUPS, CHUNK), 0.02),
        "gn_g": 1.0 + nrm(ks[13], (DEPTH, D_R), 0.02),
        "w_out": nrm(ks[14], (DEPTH, D_MODEL, D_MODEL), D_MODEL ** -0.5),
        "norm2_g": 1.0 + nrm(ks[15], (DEPTH, D_MODEL), 0.02),
        "w_ff1": nrm(ks[16], (DEPTH, D_MODEL, D_FF), D_MODEL ** -0.5),
        "w_ff2": nrm(ks[17], (DEPTH, D_FF, D_MODEL), D_FF ** -0.5),
        "final_g": 1.0 + nrm(ks[18], (D_MODEL,), 0.02),
    }


def reference(x_prompt, x_sample, state_ret, c_prompt, c_sample, w_ada, b_ada, norm1_g, w_in,
              ln_v_g, ln_v_b, w_s, b_s, gn_g, w_out, norm2_g, w_ff1, w_ff2, final_g):
    y_prompt, new_ret_prompt, _ = trunk(
        x_prompt, c_prompt, 0.0, None, w_ada, b_ada, norm1_g, w_in, ln_v_g, ln_v_b,
        w_s, b_s, gn_g, w_out, norm2_g, w_ff1, w_ff2, final_g)
    y_sample, new_ret_sample, new_chunk_v_sample = trunk(
        x_sample, c_sample, float(PAST_LEN), state_ret, w_ada, b_ada, norm1_g, w_in, ln_v_g, ln_v_b,
        w_s, b_s, gn_g, w_out, norm2_g, w_ff1, w_ff2, final_g)
    return (y_prompt, y_sample, new_ret_prompt, new_ret_sample, new_chunk_v_sample)
```

```python
import functools

import jax
import jax.numpy as jnp
from jax import lax
from jax.experimental import pallas as pl
from jax.experimental.pallas import tpu as pltpu

F32 = jnp.float32
BF16 = jnp.bfloat16

D_MODEL = 1024
CHUNK = 128
GROUPS = 4
GROUP_DIM = D_MODEL // GROUPS
HEADS = 4
HEAD_DIM = D_MODEL // HEADS
D_FF = 4 * D_MODEL
N_SLABS = 8
SLAB_U, SLAB_VA, SLAB_Q, SLAB_K, SLAB_VR, SLAB_G, SLAB_GA, SLAB_GB = range(N_SLABS)
PAST_LEN = 16384
ROPE_BASE = 10000.0
EPS = 1e-6

TILE_M = 512
ADA_BLOCK_N = 1536
STATE_BLOCK_B = 8
VMEM_LIMIT_BYTES = 56 * 1024 * 1024


def _resident(block_shape, index_map):
    return pl.BlockSpec(block_shape, index_map, pipeline_mode=pl.Buffered(1))


def _rms(x, g):
    return x * lax.rsqrt(jnp.mean(x * x, axis=-1, keepdims=True) + EPS) * g


def _center_scale(x):
    mu = jnp.mean(x, axis=-1, keepdims=True)
    xc = x - mu
    var = jnp.mean(xc * xc, axis=-1, keepdims=True)
    return xc * lax.rsqrt(var + EPS)


def _dot(a, b):
    return jnp.dot(a, b, preferred_element_type=F32)


def _col(i):
    return slice(i * D_MODEL, (i + 1) * D_MODEL)


def _ada_kernel(c_ref, w_ref, b_ref, o_ref):
    o_ref[...] = _dot(c_ref[...].astype(BF16), w_ref[...].astype(BF16)) + b_ref[...]


def _ada_call(c_all, w_ada, b_ada):
    depth, d, n = w_ada.shape
    rows = c_all.shape[0]
    return pl.pallas_call(
        _ada_kernel,
        out_shape=jax.ShapeDtypeStruct((depth, rows, n), F32),
        grid=(depth, n // ADA_BLOCK_N),
        in_specs=[
            pl.BlockSpec((rows, d), lambda l, j: (0, 0)),
            pl.BlockSpec((None, d, ADA_BLOCK_N), lambda l, j: (l, 0, j)),
            pl.BlockSpec((None, 1, ADA_BLOCK_N), lambda l, j: (l, 0, j)),
        ],
        out_specs=pl.BlockSpec((None, rows, ADA_BLOCK_N), lambda l, j: (l, 0, j)),
        compiler_params=pltpu.CompilerParams(
            dimension_semantics=("arbitrary", "arbitrary"), vmem_limit_bytes=VMEM_LIMIT_BYTES),
        name="ada",
    )(c_all, w_ada, b_ada.reshape(depth, 1, n))


def _ffn_body(x, sh2, sc2, gt2, n2g, w1_ref, w2_ref, h_s):
    h_s[...] = (_rms(x, n2g) * (1.0 + sc2) + sh2).astype(BF16)
    acc = None
    for j in range(D_FF // D_MODEL):
        hid = _dot(h_s[...], w1_ref[:, _col(j)])
        hid = jnp.square(jnp.maximum(hid, 0.0)).astype(BF16)
        part = _dot(hid, w2_ref[_col(j), :])
        acc = part if acc is None else acc + part
    return x + gt2 * acc


def _ffn_kernel(x_ref, mod_ref, n2g_ref, fg_ref, w1_ref, w2_ref, o_ref, h_s, *, final):
    y = _ffn_body(x_ref[...], mod_ref[:, _col(3)], mod_ref[:, _col(4)], mod_ref[:, _col(5)],
                  n2g_ref[...], w1_ref, w2_ref, h_s)
    if final:
        y = _rms(y, fg_ref[...])
    o_ref[...] = y


def _ffn_call(x, mod, n2g, fg, w1, w2, *, final):
    b, t, d = x.shape
    tm = min(TILE_M, t)
    mod_rows = mod.shape[1]
    return pl.pallas_call(
        functools.partial(_ffn_kernel, final=final),
        out_shape=jax.ShapeDtypeStruct(x.shape, F32),
        grid=(b, t // tm),
        in_specs=[
            pl.BlockSpec((None, tm, d), lambda i, j: (i, j, 0)),
            pl.BlockSpec((None, mod_rows, 6 * d), lambda i, j: (i, 0, 0)),
            _resident((1, d), lambda i, j: (0, 0)),
            _resident((1, d), lambda i, j: (0, 0)),
            _resident((d, D_FF), lambda i, j: (0, 0)),
            _resident((D_FF, d), lambda i, j: (0, 0)),
        ],
        out_specs=pl.BlockSpec((None, tm, d), lambda i, j: (i, j, 0)),
        scratch_shapes=[pltpu.VMEM((tm, d), BF16)],
        compiler_params=pltpu.CompilerParams(
            dimension_semantics=("arbitrary", "arbitrary"), vmem_limit_bytes=VMEM_LIMIT_BYTES),
        name="ffn",
    )(x, mod, n2g, fg, w1, w2)


def _mix_kernel(gl_ref, x_ref, mod_ref, cos_ref, sin_ref, n1g_ref, lng_ref, lnb_ref, gng_ref,
                ws_ref, bst_ref, decay_ref, xi_ref, zeta_ref, win_ref, wout_ref,
                xo_ref, s_ref,
                h_s, p_s, van_s, q_s, k_s, kz_s, v_s, a_s, on_s):
    n_chunks = x_ref.shape[0] // CHUNK

    @pl.when(pl.program_id(1) == 0)
    def _():
        s_ref[...] = jnp.zeros_like(s_ref)

    sh1 = mod_ref[:, _col(0)]
    sc1 = mod_ref[:, _col(1)]
    gt1 = mod_ref[:, _col(2)]
    h_s[...] = (_rms(x_ref[...], n1g_ref[...]) * (1.0 + sc1) + sh1).astype(BF16)

    def proj(slab):
        return _dot(h_s[...], win_ref[:, _col(slab)])

    def rows(c):
        return slice(c * CHUNK, (c + 1) * CHUNK)

    van_s[...] = (_center_scale(proj(SLAB_VA)) * lng_ref[...] + lnb_ref[...]).astype(BF16)
    p_s[...] = proj(SLAB_U)
    tril = (lax.broadcasted_iota(jnp.int32, (CHUNK, CHUNK), 0)
            >= lax.broadcasted_iota(jnp.int32, (CHUNK, CHUNK), 1))
    for g in range(GROUPS):
        w_g = jnp.where(tril, ws_ref[g], 0.0).astype(BF16)
        gc = slice(g * GROUP_DIM, (g + 1) * GROUP_DIM)
        for c in range(n_chunks):
            z = _dot(w_g, van_s[rows(c), gc]) + bst_ref[:, g:g + 1]
            a_s[rows(c), gc] = p_s[rows(c), gc] * z
    a_s[...] = jax.nn.sigmoid(proj(SLAB_GA)) * a_s[...]

    half = HEAD_DIM // 2

    def rotate(c, hd):
        lo = slice(hd * HEAD_DIM, hd * HEAD_DIM + half)
        hi = slice(hd * HEAD_DIM + half, (hd + 1) * HEAD_DIM)
        x1 = p_s[rows(c), lo]
        x2 = p_s[rows(c), hi]
        cos = cos_ref[rows(c), :]
        sin = sin_ref[rows(c), :]
        return lo, hi, x1 * cos - x2 * sin, x1 * sin + x2 * cos

    p_s[...] = proj(SLAB_Q)
    for c in range(n_chunks):
        for hd in range(HEADS):
            lo, hi, r1, r2 = rotate(c, hd)
            q_s[rows(c), lo] = r1.astype(BF16)
            q_s[rows(c), hi] = r2.astype(BF16)
    p_s[...] = proj(SLAB_K)
    k_scale = HEAD_DIM ** -0.5
    for c in range(n_chunks):
        for hd in range(HEADS):
            lo, hi, r1, r2 = rotate(c, hd)
            r1 = r1 * k_scale
            r2 = r2 * k_scale
            k_s[rows(c), lo] = r1.astype(BF16)
            k_s[rows(c), hi] = r2.astype(BF16)
            kz_s[rows(c), lo] = (r1 * zeta_ref[:, lo]).astype(BF16)
            kz_s[rows(c), hi] = (r2 * zeta_ref[:, hi]).astype(BF16)
    v_s[...] = proj(SLAB_VR).astype(BF16)

    for c in range(n_chunks):
        for hd in range(HEADS):
            hc = slice(hd * HEAD_DIM, (hd + 1) * HEAD_DIM)
            qc = q_s[rows(c), hc]
            vc = v_s[rows(c), hc]
            state = s_ref[hd]
            sc = lax.dot_general(qc, k_s[rows(c), hc], (((1,), (1,)), ((), ())),
                                 preferred_element_type=F32) * decay_ref[hd]
            o = _dot(sc.astype(BF16), vc) + _dot(qc, state.astype(BF16)) * xi_ref[:, hc]
            s_ref[hd] = state * gl_ref[hd] + lax.dot_general(
                kz_s[rows(c), hc], vc, (((0,), (0,)), ((), ())), preferred_element_type=F32)
            on_s[rows(c), hc] = _center_scale(o) * gng_ref[:, hc]

    on_s[...] = jax.nn.silu(proj(SLAB_G)) * on_s[...]
    h_s[...] = (a_s[...] + jax.nn.sigmoid(proj(SLAB_GB)) * on_s[...]).astype(BF16)
    xo_ref[...] = x_ref[...] + gt1 * _dot(h_s[...], wout_ref[...])


def _mix_call(x, mod, cos, sin, n1g, lng, lnb, gng, ws, bst, decay, xi, zeta, gl, w_in, w_out):
    b, t, d = x.shape
    tm = TILE_M
    const2 = lambda i, j: (0, 0)
    const3 = lambda i, j: (0, 0, 0)
    return pl.pallas_call(
        _mix_kernel,
        out_shape=(jax.ShapeDtypeStruct(x.shape, F32),
                   jax.ShapeDtypeStruct((b, HEADS, HEAD_DIM, HEAD_DIM), F32)),
        grid=(b, t // tm),
        in_specs=[
            pl.BlockSpec(memory_space=pltpu.SMEM),
            pl.BlockSpec((None, tm, d), lambda i, j: (i, j, 0)),
            pl.BlockSpec((None, 1, 6 * d), lambda i, j: (i, 0, 0)),
            pl.BlockSpec((tm, HEAD_DIM // 2), lambda i, j: (j, 0)),
            pl.BlockSpec((tm, HEAD_DIM // 2), lambda i, j: (j, 0)),
            _resident((1, d), const2),
            _resident((1, d), const2),
            _resident((1, d), const2),
            _resident((1, d), const2),
            _resident((GROUPS, CHUNK, CHUNK), const3),
            _resident((CHUNK, GROUPS), const2),
            _resident((HEADS, CHUNK, CHUNK), const3),
            _resident((CHUNK, d), const2),
            _resident((CHUNK, d), const2),
            _resident((d, N_SLABS * d), const2),
            _resident((d, d), const2),
        ],
        out_specs=(
            pl.BlockSpec((None, tm, d), lambda i, j: (i, j, 0)),
            pl.BlockSpec((None, HEADS, HEAD_DIM, HEAD_DIM), lambda i, j: (i, 0, 0, 0)),
        ),
        scratch_shapes=[
            pltpu.VMEM((tm, d), BF16),
            pltpu.VMEM((tm, d), F32),
            pltpu.VMEM((tm, d), BF16),
            pltpu.VMEM((tm, d), BF16),
            pltpu.VMEM((tm, d), BF16),
            pltpu.VMEM((tm, d), BF16),
            pltpu.VMEM((tm, d), BF16),
            pltpu.VMEM((tm, d), F32),
            pltpu.VMEM((tm, d), F32),
        ],
        compiler_params=pltpu.CompilerParams(
            dimension_semantics=("arbitrary", "arbitrary"), vmem_limit_bytes=VMEM_LIMIT_BYTES),
        name="mix",
    )(gl, x, mod, cos, sin, n1g, lng, lnb, gng, ws, bst, decay, xi, zeta, w_in, w_out)


def _sproj_kernel(x_ref, mod_ref, cos_ref, sin_ref, n1g_ref, lng_ref, lnb_ref, w0_ref, b0_ref,
                  win_ref, ma_ref, van_ref, q_ref, k_ref, v_ref, g_ref, gb_ref, h_s):
    sh1 = mod_ref[:, _col(0)]
    sc1 = mod_ref[:, _col(1)]
    h_s[...] = (_rms(x_ref[...], n1g_ref[...]) * (1.0 + sc1) + sh1).astype(BF16)

    def proj(slab):
        return _dot(h_s[...], win_ref[:, _col(slab)])

    van = _center_scale(proj(SLAB_VA)) * lng_ref[...] + lnb_ref[...]
    van_ref[...] = van
    z = van * w0_ref[...] + b0_ref[...]
    ma_ref[...] = jax.nn.sigmoid(proj(SLAB_GA)) * (proj(SLAB_U) * z)

    half = HEAD_DIM // 2
    cos = cos_ref[...]
    sin = sin_ref[...]

    def rotary_to(dst_ref, p, scale):
        for hd in range(HEADS):
            lo = slice(hd * HEAD_DIM, hd * HEAD_DIM + half)
            hi = slice(hd * HEAD_DIM + half, (hd + 1) * HEAD_DIM)
            x1 = p[:, lo]
            x2 = p[:, hi]
            dst_ref[:, lo] = (x1 * cos - x2 * sin) * scale
            dst_ref[:, hi] = (x1 * sin + x2 * cos) * scale

    rotary_to(q_ref, proj(SLAB_Q), 1.0)
    rotary_to(k_ref, proj(SLAB_K), HEAD_DIM ** -0.5)
    v_ref[...] = proj(SLAB_VR)
    g_ref[...] = proj(SLAB_G)
    gb_ref[...] = proj(SLAB_GB)


def _sproj_call(x, mod, cos, sin, n1g, lng, lnb, w0, b0, w_in):
    n, d = x.shape
    out = jax.ShapeDtypeStruct((n, d), F32)
    return pl.pallas_call(
        _sproj_kernel,
        out_shape=(out,) * 7,
        scratch_shapes=[pltpu.VMEM((n, d), BF16)],
        compiler_params=pltpu.CompilerParams(vmem_limit_bytes=VMEM_LIMIT_BYTES),
        name="sproj",
    )(x, mod, cos, sin, n1g, lng, lnb, w0, b0, w_in)


def _sstate_kernel(c_ref, q_ref, k_ref, v_ref, s_ref, *rest):
    o_ref, sn_ref = rest[-2], rest[-1]
    block_b = q_ref.shape[0]
    for hd in range(HEADS):
        hc = slice(hd * HEAD_DIM, (hd + 1) * HEAD_DIM)
        q = q_ref[:, hc]
        k = k_ref[:, hc]
        v = v_ref[:, hc]
        decay, xi, zeta, g_l = c_ref[0, hd], c_ref[1, hd], c_ref[2, hd], c_ref[3, hd]
        sc = jnp.sum(q * k, axis=-1, keepdims=True) * decay
        pad = jnp.zeros((CHUNK - block_b, HEAD_DIM), F32)
        q_t = jnp.concatenate([q, pad], axis=0).T
        kz_t = jnp.concatenate([k * zeta, pad], axis=0).T
        qs_rows = []
        for b in range(block_b):
            state = s_ref[b, hd]
            qs_rows.append(jnp.sum(q_t[:, b:b + 1] * state, axis=0, keepdims=True))
            sn_ref[0, b, hd] = state * g_l + kz_t[:, b:b + 1] * v[b:b + 1, :]
        o_ref[:, hc] = sc * v + jnp.concatenate(qs_rows, axis=0) * xi


def _sstate_call(consts, q, k, v, state_all, new_state_all, layer):
    n, d = q.shape
    bb = STATE_BLOCK_B
    row_spec = pl.BlockSpec((bb, d), lambda i: (i, 0))
    state_block = (None, bb, HEADS, HEAD_DIM, HEAD_DIM)
    in_specs = [
        pl.BlockSpec(memory_space=pltpu.SMEM),
        row_spec, row_spec, row_spec,
        pl.BlockSpec(state_block, lambda i: (layer, i, 0, 0, 0)),
    ]
    args = [consts, q, k, v, state_all]
    aliases = {}
    if new_state_all is not None:
        in_specs.append(pl.BlockSpec(memory_space=pl.ANY))
        args.append(new_state_all)
        aliases = {5: 1}
    return pl.pallas_call(
        _sstate_kernel,
        out_shape=(jax.ShapeDtypeStruct((n, d), F32),
                   jax.ShapeDtypeStruct(state_all.shape, F32)),
        grid=(n // bb,),
        in_specs=in_specs,
        out_specs=(
            row_spec,
            pl.BlockSpec((1, bb, HEADS, HEAD_DIM, HEAD_DIM), lambda i: (layer, i, 0, 0, 0)),
        ),
        input_output_aliases=aliases,
        compiler_params=pltpu.CompilerParams(
            dimension_semantics=("arbitrary",), vmem_limit_bytes=VMEM_LIMIT_BYTES),
        name="sstate",
    )(*args)


def _spost_kernel(o_ref, g_ref, gb_ref, ma_ref, x_ref, mod_ref, gng_ref, n2g_ref, fg_ref,
                  wout_ref, w1_ref, w2_ref, y_ref, h_s, *, final):
    for hd in range(HEADS):
        hc = slice(hd * HEAD_DIM, (hd + 1) * HEAD_DIM)
        r = jax.nn.silu(g_ref[:, hc]) * (_center_scale(o_ref[:, hc]) * gng_ref[:, hc])
        h_s[:, hc] = (ma_ref[:, hc] + jax.nn.sigmoid(gb_ref[:, hc]) * r).astype(BF16)
    x1 = x_ref[...] + mod_ref[:, _col(2)] * _dot(h_s[...], wout_ref[...])
    y = _ffn_body(x1, mod_ref[:, _col(3)], mod_ref[:, _col(4)], mod_ref[:, _col(5)],
                  n2g_ref[...], w1_ref, w2_ref, h_s)
    if final:
        y = _rms(y, fg_ref[...])
    y_ref[...] = y


def _spost_call(o, g, gb, ma, x, mod, gng, n2g, fg, w_out, w1, w2, *, final):
    n, d = x.shape
    return pl.pallas_call(
        functools.partial(_spost_kernel, final=final),
        out_shape=jax.ShapeDtypeStruct((n, d), F32),
        scratch_shapes=[pltpu.VMEM((n, d), BF16)],
        compiler_params=pltpu.CompilerParams(vmem_limit_bytes=VMEM_LIMIT_BYTES),
        name="spost",
    )(o, g, gb, ma, x, mod, gng, n2g, fg, w_out, w1, w2)


def _rope_tables(pos0, t):
    half = HEAD_DIM // 2
    inv = ROPE_BASE ** (-jnp.arange(half, dtype=F32) / half)
    pos = pos0 + jnp.arange(t, dtype=F32)
    ang = pos[:, None] * inv[None, :]
    return jnp.cos(ang), jnp.sin(ang)


def _retention_tables(length):
    log_g = jnp.log1p(-jnp.exp2(-5.0 - jnp.arange(HEADS, dtype=F32)))
    idx = jnp.arange(length, dtype=F32)
    diff = idx[:, None] - idx[None, :]
    decay = jnp.where(diff[None] >= 0.0,
                      jnp.exp(jnp.maximum(diff, 0.0)[None] * log_g[:, None, None]), 0.0)
    xi = jnp.exp((idx[:, None] + 1.0) * log_g[None, :])
    zeta = jnp.exp((length - 1.0 - idx)[:, None] * log_g[None, :])
    g_len = jnp.exp(length * log_g)
    return decay, xi, zeta, g_len


def kernel(x_prompt, x_sample, state_ret, c_prompt, c_sample, w_ada, b_ada, norm1_g, w_in, ln_v_g,
           ln_v_b, w_s, b_s, gn_g, w_out, norm2_g, w_ff1, w_ff2, final_g):
    depth = w_in.shape[0]
    batch, seq, d = x_prompt.shape
    n_sample = x_sample.shape[0]
    assert d == D_MODEL and seq % TILE_M == 0 and x_sample.shape[1] == 1
    assert n_sample % STATE_BLOCK_B == 0

    w_in_b = w_in.astype(BF16)
    w_out_b = w_out.astype(BF16)
    w_ff1_b = w_ff1.astype(BF16)
    w_ff2_b = w_ff2.astype(BF16)
    row = lambda p: p.reshape(depth, 1, d)
    n1g, lng, lnb, gng, n2g = row(norm1_g), row(ln_v_g), row(ln_v_b), row(gn_g), row(norm2_g)
    fg = final_g.reshape(1, d)

    mod = _ada_call(jnp.concatenate([c_prompt, c_sample], axis=0), w_ada, b_ada)
    mod_p = mod[:, :batch].reshape(depth, batch, 1, 6 * d)
    mod_s = mod[:, batch:]

    cos_p, sin_p = _rope_tables(0.0, seq)
    cos_s, sin_s = _rope_tables(float(PAST_LEN), 1)
    decay, xi, zeta, g_len = _retention_tables(CHUNK)
    xi_cols = jnp.repeat(xi, HEAD_DIM, axis=1)
    zeta_cols = jnp.repeat(zeta, HEAD_DIM, axis=1)
    decay1, xi1, zeta1, g_len1 = _retention_tables(1)
    consts1 = jnp.stack([decay1[:, 0, 0], xi1[0], zeta1[0], g_len1])
    b_s_t = jnp.swapaxes(b_s, 1, 2)
    w0_cols = jnp.repeat(w_s[:, :, 0, 0], GROUP_DIM, axis=1).reshape(depth, 1, d)
    b0_cols = jnp.repeat(b_s[:, :, 0], GROUP_DIM, axis=1).reshape(depth, 1, d)

    xp = x_prompt
    xs = x_sample.reshape(n_sample, d)
    ret_prompt, v_rows = [], []
    new_state = None
    for l in range(depth):
        final = l == depth - 1
        xp, ret_l = _mix_call(xp, mod_p[l], cos_p, sin_p, n1g[l], lng[l], lnb[l], gng[l], w_s[l],
                              b_s_t[l], decay, xi_cols, zeta_cols, g_len, w_in_b[l], w_out_b[l])
        xp = _ffn_call(xp, mod_p[l], n2g[l], fg, w_ff1_b[l], w_ff2_b[l], final=final)
        ret_prompt.append(ret_l)

        ma, van, q, k, v, g, gb = _sproj_call(xs, mod_s[l], cos_s, sin_s, n1g[l], lng[l], lnb[l],
                                              w0_cols[l], b0_cols[l], w_in_b[l])
        o, new_state = _sstate_call(consts1, q, k, v, state_ret, new_state, l)
        xs = _spost_call(o, g, gb, ma, xs, mod_s[l], gng[l], n2g[l], fg, w_out_b[l], w_ff1_b[l],
                         w_ff2_b[l], final=final)
        v_rows.append(van)

    return (xp, xs.reshape(x_sample.shape), jnp.stack(ret_prompt), new_state,
            jnp.stack(v_rows).reshape(depth, n_sample, 1, d))
```

```python
import functools

import jax
import jax.numpy as jnp
from jax import lax
from jax.experimental import pallas as pl
from jax.experimental.pallas import tpu as pltpu

F32 = jnp.float32
BF16 = jnp.bfloat16

D_MODEL = 1024
CHUNK = 128
GROUPS = 4
GROUP_DIM = D_MODEL // GROUPS
HEADS = 4
HEAD_DIM = D_MODEL // HEADS
D_FF = 4 * D_MODEL
N_SLABS = 8
SLAB_U, SLAB_VA, SLAB_Q, SLAB_K, SLAB_VR, SLAB_G, SLAB_GA, SLAB_GB = range(N_SLABS)
PAST_LEN = 16384
ROPE_BASE = 10000.0
EPS = 1e-6

TILE_M = 512
ADA_BLOCK_N = 1536
VMEM_LIMIT_BYTES = 56 * 1024 * 1024


def _resident(block_shape, index_map):
    return pl.BlockSpec(block_shape, index_map, pipeline_mode=pl.Buffered(1))


def _layer_row(layer):
    return _resident((None, 1, D_MODEL), lambda *_: (layer, 0, 0))


def _layer_matrix(layer, rows, cols):
    return _resident((None, rows, cols), lambda *_: (layer, 0, 0))


def _full(shape):
    return pl.BlockSpec(shape, lambda *_: (0,) * len(shape))


def _drop_first_ref(body):
    def wrapped(_, *refs):
        return body(*refs)
    return wrapped


def _call_carrying(body, carried, carried_out, *, in_specs, args, **kwargs):
    if carried is None:
        return pl.pallas_call(body, in_specs=in_specs, **kwargs)(*args)
    return pl.pallas_call(
        _drop_first_ref(body), in_specs=[pl.BlockSpec(memory_space=pl.ANY)] + in_specs,
        input_output_aliases={0: carried_out}, **kwargs)(carried, *args)


def _rms(x, g):
    return x * lax.rsqrt(jnp.mean(x * x, axis=-1, keepdims=True) + EPS) * g


def _center_scale(x):
    mu = jnp.mean(x, axis=-1, keepdims=True)
    xc = x - mu
    var = jnp.mean(xc * xc, axis=-1, keepdims=True)
    return xc * lax.rsqrt(var + EPS)


def _dot(a, b):
    return jnp.dot(a, b, preferred_element_type=F32)


def _col(i):
    return slice(i * D_MODEL, (i + 1) * D_MODEL)


def _head(hd):
    return slice(hd * HEAD_DIM, (hd + 1) * HEAD_DIM)


def _ada_kernel(cp_ref, cs_ref, w_ref, b_ref, op_ref, os_ref):
    w = w_ref[...].astype(BF16)
    op_ref[...] = _dot(cp_ref[...].astype(BF16), w) + b_ref[...]
    os_ref[...] = _dot(cs_ref[...].astype(BF16), w) + b_ref[...]


def _ada_call(c_prompt, c_sample, w_ada, b_ada):
    depth, d, n = w_ada.shape
    bp, bs = c_prompt.shape[0], c_sample.shape[0]
    return pl.pallas_call(
        _ada_kernel,
        out_shape=(jax.ShapeDtypeStruct((depth, bp, n), F32),
                   jax.ShapeDtypeStruct((depth, bs, n), F32)),
        grid=(depth, n // ADA_BLOCK_N),
        in_specs=[
            _full((bp, d)),
            _full((bs, d)),
            pl.BlockSpec((None, d, ADA_BLOCK_N), lambda l, j: (l, 0, j)),
            pl.BlockSpec((None, 1, ADA_BLOCK_N), lambda l, j: (l, 0, j)),
        ],
        out_specs=(pl.BlockSpec((None, bp, ADA_BLOCK_N), lambda l, j: (l, 0, j)),
                   pl.BlockSpec((None, bs, ADA_BLOCK_N), lambda l, j: (l, 0, j))),
        compiler_params=pltpu.CompilerParams(
            dimension_semantics=("arbitrary", "arbitrary"), vmem_limit_bytes=VMEM_LIMIT_BYTES),
        name="ada",
    )(c_prompt, c_sample, w_ada, b_ada.reshape(depth, 1, n))


def _retention_one_token(c_ref, q_ref, k_ref, v_ref, s_ref, o_ref, sn_ref):
    n_rows = q_ref.shape[0]
    for hd in range(HEADS):
        q = q_ref[:, _head(hd)]
        k = k_ref[:, _head(hd)]
        v = v_ref[:, _head(hd)]
        decay, xi, zeta, g_l = c_ref[0, hd], c_ref[1, hd], c_ref[2, hd], c_ref[3, hd]
        sc = jnp.sum(q * k, axis=-1, keepdims=True) * decay
        pad = jnp.zeros((CHUNK - n_rows, HEAD_DIM), F32)
        q_t = jnp.concatenate([q, pad], axis=0).T
        kz_t = jnp.concatenate([k * zeta, pad], axis=0).T
        qs_rows = []
        for b in range(n_rows):
            state = s_ref[b, hd]
            qs_rows.append(jnp.sum(q_t[:, b:b + 1] * state, axis=0, keepdims=True))
            sn_ref[b, hd] = state * g_l + kz_t[:, b:b + 1] * v[b:b + 1, :]
        o_ref[:, _head(hd)] = sc * v + jnp.concatenate(qs_rows, axis=0) * xi


def _ffn_body(x, sh2, sc2, gt2, n2g, w1_ref, w2_ref, h_s):
    h_s[...] = (_rms(x, n2g) * (1.0 + sc2) + sh2).astype(BF16)
    acc = None
    for j in range(D_FF // D_MODEL):
        hid = _dot(h_s[...], w1_ref[:, _col(j)])
        hid = jnp.square(jnp.maximum(hid, 0.0)).astype(BF16)
        part = _dot(hid, w2_ref[_col(j), :])
        acc = part if acc is None else acc + part
    return x + gt2 * acc


def _ffn_kernel(c_ref, x_ref, mod_ref, n2g_ref, fg_ref, w1_ref, w2_ref, q_ref, k_ref, v_ref, s_ref,
                y_ref, o_ref, sn_ref, h_s, *, final):
    _retention_one_token(c_ref, q_ref, k_ref, v_ref, s_ref, o_ref, sn_ref)
    y = _ffn_body(x_ref[...], mod_ref[:, _col(3)], mod_ref[:, _col(4)], mod_ref[:, _col(5)],
                  n2g_ref[...], w1_ref, w2_ref, h_s)
    if final:
        y = _rms(y, fg_ref[...])
    y_ref[...] = y


def _ffn_call(layer, x, mod, n2g, fg, w1, w2, consts1, q, k, v, state_all, new_state_all, *, final):
    b, t, d = x.shape
    tm = TILE_M
    nj = t // tm
    n_steps, nb, _ = q.shape
    assert n_steps == b * nj
    step = lambda i, j: i * nj + j
    rows_spec = pl.BlockSpec((None, nb, d), lambda i, j: (step(i, j), 0, 0))
    state_spec = pl.BlockSpec((None, nb, HEADS, HEAD_DIM, HEAD_DIM),
                              lambda i, j: (layer, step(i, j), 0, 0, 0))
    return _call_carrying(
        functools.partial(_ffn_kernel, final=final), new_state_all, 2,
        out_shape=(jax.ShapeDtypeStruct(x.shape, F32),
                   jax.ShapeDtypeStruct(q.shape, F32),
                   jax.ShapeDtypeStruct(state_all.shape, F32)),
        grid=(b, nj),
        in_specs=[
            pl.BlockSpec(memory_space=pltpu.SMEM),
            pl.BlockSpec((None, tm, d), lambda i, j: (i, j, 0)),
            pl.BlockSpec((None, None, 1, 6 * d), lambda i, j: (layer, i, 0, 0)),
            _layer_row(layer),
            _full((1, d)),
            _layer_matrix(layer, d, D_FF),
            _layer_matrix(layer, D_FF, d),
            rows_spec, rows_spec, rows_spec,
            state_spec,
        ],
        args=[consts1, x, mod, n2g, fg, w1, w2, q, k, v, state_all],
        out_specs=(pl.BlockSpec((None, tm, d), lambda i, j: (i, j, 0)), rows_spec, state_spec),
        scratch_shapes=[pltpu.VMEM((tm, d), BF16)],
        compiler_params=pltpu.CompilerParams(
            dimension_semantics=("arbitrary", "arbitrary"), vmem_limit_bytes=VMEM_LIMIT_BYTES),
        name="ffn",
    )


def _mix_kernel(gl_ref, x_ref, mod_ref, cos_ref, sin_ref, n1g_ref, lng_ref, lnb_ref, gng_ref,
                ws_ref, bst_ref, decay_ref, xi_ref, zeta_ref, win_ref, wout_ref,
                xo_ref, s_ref,
                h_s, p_s, van_s, q_s, k_s, kz_s, v_s, a_s, on_s):
    n_chunks = x_ref.shape[0] // CHUNK

    @pl.when(pl.program_id(1) == 0)
    def _():
        s_ref[...] = jnp.zeros_like(s_ref)

    sh1 = mod_ref[:, _col(0)]
    sc1 = mod_ref[:, _col(1)]
    gt1 = mod_ref[:, _col(2)]
    h_s[...] = (_rms(x_ref[...], n1g_ref[...]) * (1.0 + sc1) + sh1).astype(BF16)

    def proj(slab):
        return _dot(h_s[...], win_ref[:, _col(slab)])

    def rows(c):
        return slice(c * CHUNK, (c + 1) * CHUNK)

    van_s[...] = (_center_scale(proj(SLAB_VA)) * lng_ref[...] + lnb_ref[...]).astype(BF16)
    p_s[...] = proj(SLAB_U)
    tril = (lax.broadcasted_iota(jnp.int32, (CHUNK, CHUNK), 0)
            >= lax.broadcasted_iota(jnp.int32, (CHUNK, CHUNK), 1))
    for g in range(GROUPS):
        w_g = jnp.where(tril, ws_ref[g], 0.0).astype(BF16)
        gc = slice(g * GROUP_DIM, (g + 1) * GROUP_DIM)
        for c in range(n_chunks):
            z = _dot(w_g, van_s[rows(c), gc]) + bst_ref[:, g:g + 1]
            a_s[rows(c), gc] = p_s[rows(c), gc] * z
    a_s[...] = jax.nn.sigmoid(proj(SLAB_GA)) * a_s[...]

    half = HEAD_DIM // 2

    def rotate(c, hd):
        lo = slice(hd * HEAD_DIM, hd * HEAD_DIM + half)
        hi = slice(hd * HEAD_DIM + half, (hd + 1) * HEAD_DIM)
        x1 = p_s[rows(c), lo]
        x2 = p_s[rows(c), hi]
        cos = cos_ref[rows(c), :]
        sin = sin_ref[rows(c), :]
        return lo, hi, x1 * cos - x2 * sin, x1 * sin + x2 * cos

    p_s[...] = proj(SLAB_Q)
    for c in range(n_chunks):
        for hd in range(HEADS):
            lo, hi, r1, r2 = rotate(c, hd)
            q_s[rows(c), lo] = r1.astype(BF16)
            q_s[rows(c), hi] = r2.astype(BF16)
    p_s[...] = proj(SLAB_K)
    k_scale = HEAD_DIM ** -0.5
    for c in range(n_chunks):
        for hd in range(HEADS):
            lo, hi, r1, r2 = rotate(c, hd)
            r1 = r1 * k_scale
            r2 = r2 * k_scale
            k_s[rows(c), lo] = r1.astype(BF16)
            k_s[rows(c), hi] = r2.astype(BF16)
            kz_s[rows(c), lo] = (r1 * zeta_ref[:, lo]).astype(BF16)
            kz_s[rows(c), hi] = (r2 * zeta_ref[:, hi]).astype(BF16)
    v_s[...] = proj(SLAB_VR).astype(BF16)

    for c in range(n_chunks):
        for hd in range(HEADS):
            hc = _head(hd)
            qc = q_s[rows(c), hc]
            vc = v_s[rows(c), hc]
            state = s_ref[hd]
            sc = lax.dot_general(qc, k_s[rows(c), hc], (((1,), (1,)), ((), ())),
                                 preferred_element_type=F32) * decay_ref[hd]
            o = _dot(sc.astype(BF16), vc) + _dot(qc, state.astype(BF16)) * xi_ref[:, hc]
            s_ref[hd] = state * gl_ref[hd] + lax.dot_general(
                kz_s[rows(c), hc], vc, (((0,), (0,)), ((), ())), preferred_element_type=F32)
            on_s[rows(c), hc] = _center_scale(o) * gng_ref[:, hc]

    on_s[...] = jax.nn.silu(proj(SLAB_G)) * on_s[...]
    h_s[...] = (a_s[...] + jax.nn.sigmoid(proj(SLAB_GB)) * on_s[...]).astype(BF16)
    xo_ref[...] = x_ref[...] + gt1 * _dot(h_s[...], wout_ref[...])


def _mix_call(layer, x, mod, cos, sin, n1g, lng, lnb, gng, ws, bst, decay, xi, zeta, gl, w_in, w_out,
              ret_all):
    b, t, d = x.shape
    depth = w_in.shape[0]
    tm = TILE_M
    return _call_carrying(
        _mix_kernel, ret_all, 1,
        out_shape=(jax.ShapeDtypeStruct(x.shape, F32),
                   jax.ShapeDtypeStruct((depth, b, HEADS, HEAD_DIM, HEAD_DIM), F32)),
        grid=(b, t // tm),
        in_specs=[
            pl.BlockSpec(memory_space=pltpu.SMEM),
            pl.BlockSpec((None, tm, d), lambda i, j: (i, j, 0)),
            pl.BlockSpec((None, None, 1, 6 * d), lambda i, j: (layer, i, 0, 0)),
            pl.BlockSpec((tm, HEAD_DIM // 2), lambda i, j: (j, 0)),
            pl.BlockSpec((tm, HEAD_DIM // 2), lambda i, j: (j, 0)),
            _layer_row(layer), _layer_row(layer), _layer_row(layer), _layer_row(layer),
            _resident((None, GROUPS, CHUNK, CHUNK), lambda i, j: (layer, 0, 0, 0)),
            _layer_matrix(layer, CHUNK, GROUPS),
            _resident((HEADS, CHUNK, CHUNK), lambda i, j: (0, 0, 0)),
            _resident((CHUNK, d), lambda i, j: (0, 0)),
            _resident((CHUNK, d), lambda i, j: (0, 0)),
            _layer_matrix(layer, d, N_SLABS * d),
            _layer_matrix(layer, d, d),
        ],
        args=[gl, x, mod, cos, sin, n1g, lng, lnb, gng, ws, bst, decay, xi, zeta, w_in, w_out],
        out_specs=(
            pl.BlockSpec((None, tm, d), lambda i, j: (i, j, 0)),
            pl.BlockSpec((None, None, HEADS, HEAD_DIM, HEAD_DIM), lambda i, j: (layer, i, 0, 0, 0)),
        ),
        scratch_shapes=[
            pltpu.VMEM((tm, d), BF16),
            pltpu.VMEM((tm, d), F32),
            pltpu.VMEM((tm, d), BF16),
            pltpu.VMEM((tm, d), BF16),
            pltpu.VMEM((tm, d), BF16),
            pltpu.VMEM((tm, d), BF16),
            pltpu.VMEM((tm, d), BF16),
            pltpu.VMEM((tm, d), F32),
            pltpu.VMEM((tm, d), F32),
        ],
        compiler_params=pltpu.CompilerParams(
            dimension_semantics=("arbitrary", "arbitrary"), vmem_limit_bytes=VMEM_LIMIT_BYTES),
        name="mix",
    )


def _sproj_kernel(x_ref, mod_ref, cos_ref, sin_ref, n1g_ref, lng_ref, lnb_ref, w0_ref, b0_ref,
                  win_ref, van_ref, ma_ref, q_ref, k_ref, v_ref, g_ref, gb_ref, h_s):
    sh1 = mod_ref[:, _col(0)]
    sc1 = mod_ref[:, _col(1)]
    h_s[...] = (_rms(x_ref[...], n1g_ref[...]) * (1.0 + sc1) + sh1).astype(BF16)

    def proj(slab):
        return _dot(h_s[...], win_ref[:, _col(slab)])

    van = _center_scale(proj(SLAB_VA)) * lng_ref[...] + lnb_ref[...]
    van_ref[...] = van
    z = van * w0_ref[...] + b0_ref[...]
    ma_ref[...] = jax.nn.sigmoid(proj(SLAB_GA)) * (proj(SLAB_U) * z)

    half = HEAD_DIM // 2
    cos = cos_ref[...]
    sin = sin_ref[...]

    def rotary_to(dst_ref, p, scale):
        for hd in range(HEADS):
            lo = slice(hd * HEAD_DIM, hd * HEAD_DIM + half)
            hi = slice(hd * HEAD_DIM + half, (hd + 1) * HEAD_DIM)
            x1 = p[:, lo]
            x2 = p[:, hi]
            dst_ref[:, lo] = (x1 * cos - x2 * sin) * scale
            dst_ref[:, hi] = (x1 * sin + x2 * cos) * scale

    rotary_to(q_ref, proj(SLAB_Q), 1.0)
    rotary_to(k_ref, proj(SLAB_K), HEAD_DIM ** -0.5)
    v_ref[...] = proj(SLAB_VR)
    g_ref[...] = proj(SLAB_G)
    gb_ref[...] = proj(SLAB_GB)


def _sproj_call(layer, x, mod, cos, sin, n1g, lng, lnb, w0, b0, w_in, van_all):
    n, d = x.shape
    depth = w_in.shape[0]
    out = jax.ShapeDtypeStruct((n, d), F32)
    return _call_carrying(
        _sproj_kernel, van_all, 0,
        out_shape=(jax.ShapeDtypeStruct((depth, n, d), F32),) + (out,) * 6,
        grid=(1,),
        in_specs=[
            _full((n, d)),
            pl.BlockSpec((None, n, 6 * d), lambda i: (layer, 0, 0)),
            _full((1, HEAD_DIM // 2)), _full((1, HEAD_DIM // 2)),
            _layer_row(layer), _layer_row(layer), _layer_row(layer), _layer_row(layer),
            _layer_row(layer),
            _layer_matrix(layer, d, N_SLABS * d),
        ],
        args=[x, mod, cos, sin, n1g, lng, lnb, w0, b0, w_in],
        out_specs=(pl.BlockSpec((None, n, d), lambda i: (layer, 0, 0)),) + (_full((n, d)),) * 6,
        scratch_shapes=[pltpu.VMEM((n, d), BF16)],
        compiler_params=pltpu.CompilerParams(
            dimension_semantics=("arbitrary",), vmem_limit_bytes=VMEM_LIMIT_BYTES),
        name="sproj",
    )


def _spost_kernel(o_ref, g_ref, gb_ref, ma_ref, x_ref, mod_ref, gng_ref, n2g_ref, fg_ref,
                  wout_ref, w1_ref, w2_ref, y_ref, h_s, *, final):
    for hd in range(HEADS):
        hc = _head(hd)
        r = jax.nn.silu(g_ref[:, hc]) * (_center_scale(o_ref[:, hc]) * gng_ref[:, hc])
        h_s[:, hc] = (ma_ref[:, hc] + jax.nn.sigmoid(gb_ref[:, hc]) * r).astype(BF16)
    x1 = x_ref[...] + mod_ref[:, _col(2)] * _dot(h_s[...], wout_ref[...])
    y = _ffn_body(x1, mod_ref[:, _col(3)], mod_ref[:, _col(4)], mod_ref[:, _col(5)],
                  n2g_ref[...], w1_ref, w2_ref, h_s)
    if final:
        y = _rms(y, fg_ref[...])
    y_ref[...] = y


def _spost_call(layer, o, g, gb, ma, x, mod, gng, n2g, fg, w_out, w1, w2, *, final):
    n, d = x.shape
    return pl.pallas_call(
        functools.partial(_spost_kernel, final=final),
        out_shape=jax.ShapeDtypeStruct((n, d), F32),
        grid=(1,),
        in_specs=[
            _full((n, d)), _full((n, d)), _full((n, d)), _full((n, d)), _full((n, d)),
            pl.BlockSpec((None, n, 6 * d), lambda i: (layer, 0, 0)),
            _layer_row(layer), _layer_row(layer),
            _full((1, d)),
            _layer_matrix(layer, d, d),
            _layer_matrix(layer, d, D_FF),
            _layer_matrix(layer, D_FF, d),
        ],
        out_specs=_full((n, d)),
        scratch_shapes=[pltpu.VMEM((n, d), BF16)],
        compiler_params=pltpu.CompilerParams(
            dimension_semantics=("arbitrary",), vmem_limit_bytes=VMEM_LIMIT_BYTES),
        name="spost",
    )(o, g, gb, ma, x, mod, gng, n2g, fg, w_out, w1, w2)


def _rope_tables(pos0, t):
    half = HEAD_DIM // 2
    inv = ROPE_BASE ** (-jnp.arange(half, dtype=F32) / half)
    pos = pos0 + jnp.arange(t, dtype=F32)
    ang = pos[:, None] * inv[None, :]
    return jnp.cos(ang), jnp.sin(ang)


def _retention_tables(length):
    log_g = jnp.log1p(-jnp.exp2(-5.0 - jnp.arange(HEADS, dtype=F32)))
    idx = jnp.arange(length, dtype=F32)
    diff = idx[:, None] - idx[None, :]
    decay = jnp.where(diff[None] >= 0.0,
                      jnp.exp(jnp.maximum(diff, 0.0)[None] * log_g[:, None, None]), 0.0)
    xi = jnp.exp((idx[:, None] + 1.0) * log_g[None, :])
    zeta = jnp.exp((length - 1.0 - idx)[:, None] * log_g[None, :])
    g_len = jnp.exp(length * log_g)
    return decay, xi, zeta, g_len


def kernel(x_prompt, x_sample, state_ret, c_prompt, c_sample, w_ada, b_ada, norm1_g, w_in, ln_v_g,
           ln_v_b, w_s, b_s, gn_g, w_out, norm2_g, w_ff1, w_ff2, final_g):
    depth = w_in.shape[0]
    batch, seq, d = x_prompt.shape
    n_sample = x_sample.shape[0]
    n_steps = batch * (seq // TILE_M)
    assert d == D_MODEL and seq % TILE_M == 0 and x_sample.shape[1] == 1
    assert n_sample % n_steps == 0

    w_in_b = w_in.astype(BF16)
    w_out_b = w_out.astype(BF16)
    w_ff1_b = w_ff1.astype(BF16)
    w_ff2_b = w_ff2.astype(BF16)
    row = lambda p: p.reshape(depth, 1, d)
    n1g, lng, lnb, gng, n2g = row(norm1_g), row(ln_v_g), row(ln_v_b), row(gn_g), row(norm2_g)
    fg = final_g.reshape(1, d)

    mod_p, mod_s = _ada_call(c_prompt, c_sample, w_ada, b_ada)
    mod_p = mod_p.reshape(depth, batch, 1, 6 * d)

    cos_p, sin_p = _rope_tables(0.0, seq)
    cos_s, sin_s = _rope_tables(float(PAST_LEN), 1)
    decay, xi, zeta, g_len = _retention_tables(CHUNK)
    xi_cols = jnp.repeat(xi, HEAD_DIM, axis=1)
    zeta_cols = jnp.repeat(zeta, HEAD_DIM, axis=1)
    decay1, xi1, zeta1, g_len1 = _retention_tables(1)
    consts1 = jnp.stack([decay1[:, 0, 0], xi1[0], zeta1[0], g_len1])
    b_s_t = jnp.swapaxes(b_s, 1, 2)
    w0_cols = jnp.repeat(w_s[:, :, 0, 0], GROUP_DIM, axis=1).reshape(depth, 1, d)
    b0_cols = jnp.repeat(b_s[:, :, 0], GROUP_DIM, axis=1).reshape(depth, 1, d)

    xp = x_prompt
    xs = x_sample.reshape(n_sample, d)
    ret_prompt = new_state = van_all = None
    split = lambda a: a.reshape(n_steps, n_sample // n_steps, d)
    for l in range(depth):
        final = l == depth - 1
        xp, ret_prompt = _mix_call(l, xp, mod_p, cos_p, sin_p, n1g, lng, lnb, gng, w_s, b_s_t, decay,
                                   xi_cols, zeta_cols, g_len, w_in_b, w_out_b, ret_prompt)
        van_all, ma, q, k, v, g, gb = _sproj_call(l, xs, mod_s, cos_s, sin_s, n1g, lng, lnb, w0_cols,
                                                  b0_cols, w_in_b, van_all)
        xp, o, new_state = _ffn_call(l, xp, mod_p, n2g, fg, w_ff1_b, w_ff2_b, consts1,
                                     split(q), split(k), split(v), state_ret, new_state, final=final)
        xs = _spost_call(l, o.reshape(n_sample, d), g, gb, ma, xs, mod_s, gng, n2g, fg, w_out_b,
                         w_ff1_b, w_ff2_b, final=final)

    return (xp, xs.reshape(x_sample.shape), ret_prompt, new_state,
            van_all.reshape(depth, n_sample, 1, d))
```

```python
import functools

import jax
import jax.numpy as jnp
from jax import lax
from jax.experimental import pallas as pl
from jax.experimental.pallas import tpu as pltpu

F32 = jnp.float32
BF16 = jnp.bfloat16

D_MODEL = 1024
CHUNK = 128
GROUPS = 4
GROUP_DIM = D_MODEL // GROUPS
HEADS = 4
HEAD_DIM = D_MODEL // HEADS
D_FF = 4 * D_MODEL
N_SLABS = 8
SLAB_U, SLAB_VA, SLAB_Q, SLAB_K, SLAB_VR, SLAB_G, SLAB_GA, SLAB_GB = range(N_SLABS)
PAST_LEN = 16384
ROPE_BASE = 10000.0
EPS = 1e-6

TILE_M = 512
ADA_BLOCK_N = 1536
VMEM_LIMIT_BYTES = 56 * 1024 * 1024


def _resident(block_shape, index_map):
    return pl.BlockSpec(block_shape, index_map, pipeline_mode=pl.Buffered(1))


def _layer_row(layer):
    return _resident((None, 1, D_MODEL), lambda *_: (layer, 0, 0))


def _layer_matrix(layer, rows, cols):
    return _resident((None, rows, cols), lambda *_: (layer, 0, 0))


def _matrix(rows, cols):
    return _resident((rows, cols), lambda *_: (0, 0))


def _cast_specs(layer, rows, cols, n_steps, step):
    assert rows % n_steps == 0
    rb = rows // n_steps
    return (pl.BlockSpec((None, rb, cols), lambda *ids: (layer, step(*ids), 0)),
            pl.BlockSpec((rb, cols), lambda *ids: (step(*ids), 0)))


def _full(shape):
    return pl.BlockSpec(shape, lambda *_: (0,) * len(shape))


def _drop_first_ref(body):
    def wrapped(_, *refs):
        return body(*refs)
    return wrapped


def _call_carrying(body, carried, carried_out, *, in_specs, args, **kwargs):
    if carried is None:
        return pl.pallas_call(body, in_specs=in_specs, **kwargs)(*args)
    return pl.pallas_call(
        _drop_first_ref(body), in_specs=[pl.BlockSpec(memory_space=pl.ANY)] + in_specs,
        input_output_aliases={0: carried_out}, **kwargs)(carried, *args)


def _rms(x, g):
    return x * lax.rsqrt(jnp.mean(x * x, axis=-1, keepdims=True) + EPS) * g


def _modulated_rms(x, g, scale, shift):
    return x * lax.rsqrt(jnp.mean(x * x, axis=-1, keepdims=True) + EPS) * (g * (1.0 + scale)) + shift


def _center_scale(x):
    mu = jnp.mean(x, axis=-1, keepdims=True)
    xc = x - mu
    var = jnp.mean(xc * xc, axis=-1, keepdims=True)
    return xc * lax.rsqrt(var + EPS)


def _dot(a, b):
    return jnp.dot(a, b, preferred_element_type=F32)


def _col(i):
    return slice(i * D_MODEL, (i + 1) * D_MODEL)


def _head(hd):
    return slice(hd * HEAD_DIM, (hd + 1) * HEAD_DIM)


def _ada_kernel(cp_ref, cs_ref, w_ref, b_ref, op_ref, os_ref):
    w = w_ref[...].astype(BF16)
    op_ref[...] = _dot(cp_ref[...].astype(BF16), w) + b_ref[...]
    os_ref[...] = _dot(cs_ref[...].astype(BF16), w) + b_ref[...]


def _ada_call(c_prompt, c_sample, w_ada, b_ada):
    depth, d, n = w_ada.shape
    bp, bs = c_prompt.shape[0], c_sample.shape[0]
    return pl.pallas_call(
        _ada_kernel,
        out_shape=(jax.ShapeDtypeStruct((depth, bp, n), F32),
                   jax.ShapeDtypeStruct((depth, bs, n), F32)),
        grid=(depth, n // ADA_BLOCK_N),
        in_specs=[
            _full((bp, d)),
            _full((bs, d)),
            pl.BlockSpec((None, d, ADA_BLOCK_N), lambda l, j: (l, 0, j)),
            pl.BlockSpec((None, 1, ADA_BLOCK_N), lambda l, j: (l, 0, j)),
        ],
        out_specs=(pl.BlockSpec((None, bp, ADA_BLOCK_N), lambda l, j: (l, 0, j)),
                   pl.BlockSpec((None, bs, ADA_BLOCK_N), lambda l, j: (l, 0, j))),
        compiler_params=pltpu.CompilerParams(
            dimension_semantics=("arbitrary", "arbitrary"), vmem_limit_bytes=VMEM_LIMIT_BYTES),
        name="ada",
    )(c_prompt, c_sample, w_ada, b_ada.reshape(depth, 1, n))


def _retention_one_token(c_ref, q_ref, k_ref, v_ref, s_ref, o_ref, sn_ref):
    n_rows = q_ref.shape[0]
    for hd in range(HEADS):
        q = q_ref[:, _head(hd)]
        k = k_ref[:, _head(hd)]
        v = v_ref[:, _head(hd)]
        decay, xi, zeta, g_l = c_ref[0, hd], c_ref[1, hd], c_ref[2, hd], c_ref[3, hd]
        sc = jnp.sum(q * k, axis=-1, keepdims=True) * decay
        pad = jnp.zeros((CHUNK - n_rows, HEAD_DIM), F32)
        q_t = jnp.concatenate([q, pad], axis=0).T
        kz_t = jnp.concatenate([k * zeta, pad], axis=0).T
        qs_rows = []
        for b in range(n_rows):
            state = s_ref[b, hd]
            qs_rows.append(jnp.sum(q_t[:, b:b + 1] * state, axis=0, keepdims=True))
            sn_ref[b, hd] = state * g_l + kz_t[:, b:b + 1] * v[b:b + 1, :]
        o_ref[:, _head(hd)] = sc * v + jnp.concatenate(qs_rows, axis=0) * xi


def _ffn_body(x, sh2, sc2, gt2, n2g, w1_ref, w2_ref, h_s):
    h_s[...] = _modulated_rms(x, n2g, sc2, sh2).astype(BF16)
    acc = None
    for j in range(D_FF // D_MODEL):
        hid = _dot(h_s[...], w1_ref[:, _col(j)])
        hid = jnp.square(jnp.maximum(hid, 0.0)).astype(BF16)
        part = _dot(hid, w2_ref[_col(j), :])
        acc = part if acc is None else acc + part
    return x + gt2 * acc


def _ffn_kernel(*refs, final):
    (c_ref, x_ref, mod_ref, n2g_ref, fg_ref, w1_ref, w2_ref, q_ref, k_ref, v_ref, s_ref) = refs[:11]
    h_s = refs[-1]
    if final:
        y_ref, o_ref, sn_ref = refs[11:14]
    else:
        win_ref, wout_ref, y_ref, o_ref, sn_ref, winb_ref, woutb_ref = refs[11:18]
        winb_ref[...] = win_ref[...].astype(BF16)
        woutb_ref[...] = wout_ref[...].astype(BF16)
    _retention_one_token(c_ref, q_ref, k_ref, v_ref, s_ref, o_ref, sn_ref)
    y = _ffn_body(x_ref[...], mod_ref[:, _col(3)], mod_ref[:, _col(4)], mod_ref[:, _col(5)],
                  n2g_ref[...], w1_ref, w2_ref, h_s)
    if final:
        y = _rms(y, fg_ref[...])
    y_ref[...] = y


def _ffn_call(layer, x, mod, n2g, fg, w1, w2, consts1, q, k, v, state_all, new_state_all, w_in, w_out,
              *, final):
    b, t, d = x.shape
    tm = TILE_M
    nj = t // tm
    n_steps, nb, _ = q.shape
    assert n_steps == b * nj
    step = lambda i, j: i * nj + j
    rows_spec = pl.BlockSpec((None, nb, d), lambda i, j: (step(i, j), 0, 0))
    state_spec = pl.BlockSpec((None, nb, HEADS, HEAD_DIM, HEAD_DIM),
                              lambda i, j: (layer, step(i, j), 0, 0, 0))
    in_specs = [
        pl.BlockSpec(memory_space=pltpu.SMEM),
        pl.BlockSpec((None, tm, d), lambda i, j: (i, j, 0)),
        pl.BlockSpec((None, None, 1, 6 * d), lambda i, j: (layer, i, 0, 0)),
        _layer_row(layer),
        _full((1, d)),
        _matrix(d, D_FF),
        _matrix(D_FF, d),
        rows_spec, rows_spec, rows_spec,
        state_spec,
    ]
    args = [consts1, x, mod, n2g, fg, w1, w2, q, k, v, state_all]
    out_shape = [jax.ShapeDtypeStruct(x.shape, F32),
                 jax.ShapeDtypeStruct(q.shape, F32),
                 jax.ShapeDtypeStruct(state_all.shape, F32)]
    out_specs = [pl.BlockSpec((None, tm, d), lambda i, j: (i, j, 0)), rows_spec, state_spec]
    if not final:
        for w in (w_in, w_out):
            src, dst = _cast_specs(layer + 1, w.shape[1], w.shape[2], n_steps, step)
            in_specs.append(src)
            args.append(w)
            out_specs.append(dst)
            out_shape.append(jax.ShapeDtypeStruct(w.shape[1:], BF16))
    return _call_carrying(
        functools.partial(_ffn_kernel, final=final), new_state_all, 2,
        out_shape=tuple(out_shape),
        grid=(b, nj),
        in_specs=in_specs,
        args=args,
        out_specs=tuple(out_specs),
        scratch_shapes=[pltpu.VMEM((tm, d), BF16)],
        compiler_params=pltpu.CompilerParams(
            dimension_semantics=("arbitrary", "arbitrary"), vmem_limit_bytes=VMEM_LIMIT_BYTES),
        name="ffn",
    )


def _mix_kernel(gl_ref, x_ref, mod_ref, cos_ref, sin_ref, n1g_ref, lng_ref, lnb_ref, gng_ref,
                ws_ref, bst_ref, decay_ref, xi_ref, zeta_ref, win_ref, wout_ref, wf1_ref, wf2_ref,
                xo_ref, s_ref, wf1b_ref, wf2b_ref,
                h_s, p_s, van_s, q_s, k_s, kz_s, v_s, a_s, on_s):
    n_chunks = x_ref.shape[0] // CHUNK

    @pl.when(pl.program_id(1) == 0)
    def _():
        s_ref[...] = jnp.zeros_like(s_ref)

    wf1b_ref[...] = wf1_ref[...].astype(BF16)
    wf2b_ref[...] = wf2_ref[...].astype(BF16)

    sh1 = mod_ref[:, _col(0)]
    sc1 = mod_ref[:, _col(1)]
    gt1 = mod_ref[:, _col(2)]
    h_s[...] = _modulated_rms(x_ref[...], n1g_ref[...], sc1, sh1).astype(BF16)

    def proj(slab):
        return _dot(h_s[...], win_ref[:, _col(slab)])

    def rows(c):
        return slice(c * CHUNK, (c + 1) * CHUNK)

    van_s[...] = (_center_scale(proj(SLAB_VA)) * lng_ref[...] + lnb_ref[...]).astype(BF16)
    p_s[...] = proj(SLAB_U)
    tril = (lax.broadcasted_iota(jnp.int32, (CHUNK, CHUNK), 0)
            >= lax.broadcasted_iota(jnp.int32, (CHUNK, CHUNK), 1))
    for g in range(GROUPS):
        w_g = jnp.where(tril, ws_ref[g], 0.0).astype(BF16)
        gc = slice(g * GROUP_DIM, (g + 1) * GROUP_DIM)
        for c in range(n_chunks):
            z = _dot(w_g, van_s[rows(c), gc]) + bst_ref[:, g:g + 1]
            a_s[rows(c), gc] = p_s[rows(c), gc] * z
    a_s[...] = jax.nn.sigmoid(proj(SLAB_GA)) * a_s[...]

    half = HEAD_DIM // 2

    def rotate(c, hd):
        lo = slice(hd * HEAD_DIM, hd * HEAD_DIM + half)
        hi = slice(hd * HEAD_DIM + half, (hd + 1) * HEAD_DIM)
        x1 = p_s[rows(c), lo]
        x2 = p_s[rows(c), hi]
        cos = cos_ref[rows(c), :]
        sin = sin_ref[rows(c), :]
        return lo, hi, x1 * cos - x2 * sin, x1 * sin + x2 * cos

    p_s[...] = proj(SLAB_Q)
    for c in range(n_chunks):
        for hd in range(HEADS):
            lo, hi, r1, r2 = rotate(c, hd)
            q_s[rows(c), lo] = r1.astype(BF16)
            q_s[rows(c), hi] = r2.astype(BF16)
    p_s[...] = proj(SLAB_K)
    k_scale = HEAD_DIM ** -0.5
    for c in range(n_chunks):
        for hd in range(HEADS):
            lo, hi, r1, r2 = rotate(c, hd)
            r1 = r1 * k_scale
            r2 = r2 * k_scale
            k_s[rows(c), lo] = r1.astype(BF16)
            k_s[rows(c), hi] = r2.astype(BF16)
            kz_s[rows(c), lo] = (r1 * zeta_ref[:, lo]).astype(BF16)
            kz_s[rows(c), hi] = (r2 * zeta_ref[:, hi]).astype(BF16)
    v_s[...] = proj(SLAB_VR).astype(BF16)

    for c in range(n_chunks):
        for hd in range(HEADS):
            hc = _head(hd)
            qc = q_s[rows(c), hc]
            vc = v_s[rows(c), hc]
            state = s_ref[hd]
            sc = lax.dot_general(qc, k_s[rows(c), hc], (((1,), (1,)), ((), ())),
                                 preferred_element_type=F32) * decay_ref[hd]
            o = _dot(sc.astype(BF16), vc) + _dot(qc, state.astype(BF16)) * xi_ref[:, hc]
            s_ref[hd] = state * gl_ref[hd] + lax.dot_general(
                kz_s[rows(c), hc], vc, (((0,), (0,)), ((), ())), preferred_element_type=F32)
            on_s[rows(c), hc] = _center_scale(o) * gng_ref[:, hc]

    on_s[...] = jax.nn.silu(proj(SLAB_G)) * on_s[...]
    h_s[...] = (a_s[...] + jax.nn.sigmoid(proj(SLAB_GB)) * on_s[...]).astype(BF16)
    xo_ref[...] = x_ref[...] + gt1 * _dot(h_s[...], wout_ref[...])


def _mix_call(layer, x, mod, cos, sin, n1g, lng, lnb, gng, ws, bst, decay, xi, zeta, gl, w_in, w_out,
              w_ff1, w_ff2, ret_all):
    b, t, d = x.shape
    depth = w_ff1.shape[0]
    tm = TILE_M
    nj = t // tm
    step = lambda i, j: i * nj + j
    ff1_src, ff1_dst = _cast_specs(layer, d, D_FF, b * nj, step)
    ff2_src, ff2_dst = _cast_specs(layer, D_FF, d, b * nj, step)
    return _call_carrying(
        _mix_kernel, ret_all, 1,
        out_shape=(jax.ShapeDtypeStruct(x.shape, F32),
                   jax.ShapeDtypeStruct((depth, b, HEADS, HEAD_DIM, HEAD_DIM), F32),
                   jax.ShapeDtypeStruct((d, D_FF), BF16),
                   jax.ShapeDtypeStruct((D_FF, d), BF16)),
        grid=(b, nj),
        in_specs=[
            pl.BlockSpec(memory_space=pltpu.SMEM),
            pl.BlockSpec((None, tm, d), lambda i, j: (i, j, 0)),
            pl.BlockSpec((None, None, 1, 6 * d), lambda i, j: (layer, i, 0, 0)),
            pl.BlockSpec((tm, HEAD_DIM // 2), lambda i, j: (j, 0)),
            pl.BlockSpec((tm, HEAD_DIM // 2), lambda i, j: (j, 0)),
            _layer_row(layer), _layer_row(layer), _layer_row(layer), _layer_row(layer),
            _resident((None, GROUPS, CHUNK, CHUNK), lambda i, j: (layer, 0, 0, 0)),
            _layer_matrix(layer, CHUNK, GROUPS),
            _resident((HEADS, CHUNK, CHUNK), lambda i, j: (0, 0, 0)),
            _resident((CHUNK, d), lambda i, j: (0, 0)),
            _resident((CHUNK, d), lambda i, j: (0, 0)),
            _matrix(d, N_SLABS * d),
            _matrix(d, d),
            ff1_src, ff2_src,
        ],
        args=[gl, x, mod, cos, sin, n1g, lng, lnb, gng, ws, bst, decay, xi, zeta, w_in, w_out,
              w_ff1, w_ff2],
        out_specs=(
            pl.BlockSpec((None, tm, d), lambda i, j: (i, j, 0)),
            pl.BlockSpec((None, None, HEADS, HEAD_DIM, HEAD_DIM), lambda i, j: (layer, i, 0, 0, 0)),
            ff1_dst, ff2_dst,
        ),
        scratch_shapes=[
            pltpu.VMEM((tm, d), BF16),
            pltpu.VMEM((tm, d), F32),
            pltpu.VMEM((tm, d), BF16),
            pltpu.VMEM((tm, d), BF16),
            pltpu.VMEM((tm, d), BF16),
            pltpu.VMEM((tm, d), BF16),
            pltpu.VMEM((tm, d), BF16),
            pltpu.VMEM((tm, d), F32),
            pltpu.VMEM((tm, d), F32),
        ],
        compiler_params=pltpu.CompilerParams(
            dimension_semantics=("arbitrary", "arbitrary"), vmem_limit_bytes=VMEM_LIMIT_BYTES),
        name="mix",
    )


def _sproj_kernel(x_ref, mod_ref, cos_ref, sin_ref, n1g_ref, lng_ref, lnb_ref, w0_ref, b0_ref,
                  win_ref, van_ref, ma_ref, q_ref, k_ref, v_ref, g_ref, gb_ref, h_s):
    sh1 = mod_ref[:, _col(0)]
    sc1 = mod_ref[:, _col(1)]
    h_s[...] = _modulated_rms(x_ref[...], n1g_ref[...], sc1, sh1).astype(BF16)

    def proj(slab):
        return _dot(h_s[...], win_ref[:, _col(slab)])

    van = _center_scale(proj(SLAB_VA)) * lng_ref[...] + lnb_ref[...]
    van_ref[...] = van
    z = van * w0_ref[...] + b0_ref[...]
    ma_ref[...] = jax.nn.sigmoid(proj(SLAB_GA)) * (proj(SLAB_U) * z)

    half = HEAD_DIM // 2
    cos = cos_ref[...]
    sin = sin_ref[...]

    def rotary_to(dst_ref, p, scale):
        for hd in range(HEADS):
            lo = slice(hd * HEAD_DIM, hd * HEAD_DIM + half)
            hi = slice(hd * HEAD_DIM + half, (hd + 1) * HEAD_DIM)
            x1 = p[:, lo]
            x2 = p[:, hi]
            dst_ref[:, lo] = (x1 * cos - x2 * sin) * scale
            dst_ref[:, hi] = (x1 * sin + x2 * cos) * scale

    rotary_to(q_ref, proj(SLAB_Q), 1.0)
    rotary_to(k_ref, proj(SLAB_K), HEAD_DIM ** -0.5)
    v_ref[...] = proj(SLAB_VR)
    g_ref[...] = proj(SLAB_G)
    gb_ref[...] = proj(SLAB_GB)


def _sproj_call(layer, x, mod, cos, sin, n1g, lng, lnb, w0, b0, w_in, van_all):
    n, d = x.shape
    depth = mod.shape[0]
    out = jax.ShapeDtypeStruct((n, d), F32)
    return _call_carrying(
        _sproj_kernel, van_all, 0,
        out_shape=(jax.ShapeDtypeStruct((depth, n, d), F32),) + (out,) * 6,
        grid=(1,),
        in_specs=[
            _full((n, d)),
            pl.BlockSpec((None, n, 6 * d), lambda i: (layer, 0, 0)),
            _full((1, HEAD_DIM // 2)), _full((1, HEAD_DIM // 2)),
            _layer_row(layer), _layer_row(layer), _layer_row(layer), _layer_row(layer),
            _layer_row(layer),
            _matrix(d, N_SLABS * d),
        ],
        args=[x, mod, cos, sin, n1g, lng, lnb, w0, b0, w_in],
        out_specs=(pl.BlockSpec((None, n, d), lambda i: (layer, 0, 0)),) + (_full((n, d)),) * 6,
        scratch_shapes=[pltpu.VMEM((n, d), BF16)],
        compiler_params=pltpu.CompilerParams(
            dimension_semantics=("arbitrary",), vmem_limit_bytes=VMEM_LIMIT_BYTES),
        name="sproj",
    )


def _spost_kernel(o_ref, g_ref, gb_ref, ma_ref, x_ref, mod_ref, gng_ref, n2g_ref, fg_ref,
                  wout_ref, w1_ref, w2_ref, y_ref, h_s, *, final):
    for hd in range(HEADS):
        hc = _head(hd)
        r = jax.nn.silu(g_ref[:, hc]) * (_center_scale(o_ref[:, hc]) * gng_ref[:, hc])
        h_s[:, hc] = (ma_ref[:, hc] + jax.nn.sigmoid(gb_ref[:, hc]) * r).astype(BF16)
    x1 = x_ref[...] + mod_ref[:, _col(2)] * _dot(h_s[...], wout_ref[...])
    y = _ffn_body(x1, mod_ref[:, _col(3)], mod_ref[:, _col(4)], mod_ref[:, _col(5)],
                  n2g_ref[...], w1_ref, w2_ref, h_s)
    if final:
        y = _rms(y, fg_ref[...])
    y_ref[...] = y


def _spost_call(layer, o, g, gb, ma, x, mod, gng, n2g, fg, w_out, w1, w2, *, final):
    n, d = x.shape
    return pl.pallas_call(
        functools.partial(_spost_kernel, final=final),
        out_shape=jax.ShapeDtypeStruct((n, d), F32),
        grid=(1,),
        in_specs=[
            _full((n, d)), _full((n, d)), _full((n, d)), _full((n, d)), _full((n, d)),
            pl.BlockSpec((None, n, 6 * d), lambda i: (layer, 0, 0)),
            _layer_row(layer), _layer_row(layer),
            _full((1, d)),
            _matrix(d, d),
            _matrix(d, D_FF),
            _matrix(D_FF, d),
        ],
        out_specs=_full((n, d)),
        scratch_shapes=[pltpu.VMEM((n, d), BF16)],
        compiler_params=pltpu.CompilerParams(
            dimension_semantics=("arbitrary",), vmem_limit_bytes=VMEM_LIMIT_BYTES),
        name="spost",
    )(o, g, gb, ma, x, mod, gng, n2g, fg, w_out, w1, w2)


def _rope_tables(pos0, t):
    half = HEAD_DIM // 2
    inv = ROPE_BASE ** (-jnp.arange(half, dtype=F32) / half)
    pos = pos0 + jnp.arange(t, dtype=F32)
    ang = pos[:, None] * inv[None, :]
    return jnp.cos(ang), jnp.sin(ang)


def _retention_tables(length):
    log_g = jnp.log1p(-jnp.exp2(-5.0 - jnp.arange(HEADS, dtype=F32)))
    idx = jnp.arange(length, dtype=F32)
    diff = idx[:, None] - idx[None, :]
    decay = jnp.where(diff[None] >= 0.0,
                      jnp.exp(jnp.maximum(diff, 0.0)[None] * log_g[:, None, None]), 0.0)
    xi = jnp.exp((idx[:, None] + 1.0) * log_g[None, :])
    zeta = jnp.exp((length - 1.0 - idx)[:, None] * log_g[None, :])
    g_len = jnp.exp(length * log_g)
    return decay, xi, zeta, g_len


def kernel(x_prompt, x_sample, state_ret, c_prompt, c_sample, w_ada, b_ada, norm1_g, w_in, ln_v_g,
           ln_v_b, w_s, b_s, gn_g, w_out, norm2_g, w_ff1, w_ff2, final_g):
    depth = w_in.shape[0]
    batch, seq, d = x_prompt.shape
    n_sample = x_sample.shape[0]
    n_steps = batch * (seq // TILE_M)
    assert d == D_MODEL and seq % TILE_M == 0 and x_sample.shape[1] == 1
    assert n_sample % n_steps == 0

    w_in_b = w_in[0].astype(BF16)
    w_out_b = w_out[0].astype(BF16)
    row = lambda p: p.reshape(depth, 1, d)
    n1g, lng, lnb, gng, n2g = row(norm1_g), row(ln_v_g), row(ln_v_b), row(gn_g), row(norm2_g)
    fg = final_g.reshape(1, d)

    mod_p, mod_s = _ada_call(c_prompt, c_sample, w_ada, b_ada)
    mod_p = mod_p.reshape(depth, batch, 1, 6 * d)

    cos_p, sin_p = _rope_tables(0.0, seq)
    cos_s, sin_s = _rope_tables(float(PAST_LEN), 1)
    decay, xi, zeta, g_len = _retention_tables(CHUNK)
    xi_cols = jnp.repeat(xi, HEAD_DIM, axis=1)
    zeta_cols = jnp.repeat(zeta, HEAD_DIM, axis=1)
    decay1, xi1, zeta1, g_len1 = _retention_tables(1)
    consts1 = jnp.stack([decay1[:, 0, 0], xi1[0], zeta1[0], g_len1])
    b_s_t = jnp.swapaxes(b_s, 1, 2)
    w0_cols = jnp.repeat(w_s[:, :, 0, 0], GROUP_DIM, axis=1).reshape(depth, 1, d)
    b0_cols = jnp.repeat(b_s[:, :, 0], GROUP_DIM, axis=1).reshape(depth, 1, d)

    xp = x_prompt
    xs = x_sample.reshape(n_sample, d)
    ret_prompt = new_state = van_all = None
    split = lambda a: a.reshape(n_steps, n_sample // n_steps, d)
    for l in range(depth):
        final = l == depth - 1
        xp, ret_prompt, w_ff1_b, w_ff2_b = _mix_call(
            l, xp, mod_p, cos_p, sin_p, n1g, lng, lnb, gng, w_s, b_s_t, decay, xi_cols, zeta_cols,
            g_len, w_in_b, w_out_b, w_ff1, w_ff2, ret_prompt)
        van_all, ma, q, k, v, g, gb = _sproj_call(l, xs, mod_s, cos_s, sin_s, n1g, lng, lnb, w0_cols,
                                                  b0_cols, w_in_b, van_all)
        xp, o, new_state, *next_w = _ffn_call(
            l, xp, mod_p, n2g, fg, w_ff1_b, w_ff2_b, consts1, split(q), split(k), split(v),
            state_ret, new_state, w_in, w_out, final=final)
        xs = _spost_call(l, o.reshape(n_sample, d), g, gb, ma, xs, mod_s, gng, n2g, fg, w_out_b,
                         w_ff1_b, w_ff2_b, final=final)
        if not final:
            w_in_b, w_out_b = next_w

    return (xp, xs.reshape(x_sample.shape), ret_prompt, new_state,
            van_all.reshape(depth, n_sample, 1, d))
```

```python
import functools

import jax
import jax.numpy as jnp
from jax import lax
from jax.experimental import pallas as pl
from jax.experimental.pallas import tpu as pltpu

F32 = jnp.float32
BF16 = jnp.bfloat16

D_MODEL = 1024
CHUNK = 128
GROUPS = 4
GROUP_DIM = D_MODEL // GROUPS
HEADS = 4
HEAD_DIM = D_MODEL // HEADS
D_FF = 4 * D_MODEL
N_SLABS = 8
SLAB_U, SLAB_VA, SLAB_Q, SLAB_K, SLAB_VR, SLAB_G, SLAB_GA, SLAB_GB = range(N_SLABS)
PAST_LEN = 16384
ROPE_BASE = 10000.0
EPS = 1e-6

RET_CHUNK = 256
TILE_M = 512
ADA_BLOCK_N = 1536
VMEM_LIMIT_BYTES = 56 * 1024 * 1024


def _resident(block_shape, index_map):
    return pl.BlockSpec(block_shape, index_map, pipeline_mode=pl.Buffered(1))


def _layer_row(layer):
    return _resident((None, 1, D_MODEL), lambda *_: (layer, 0, 0))


def _layer_matrix(layer, rows, cols):
    return _resident((None, rows, cols), lambda *_: (layer, 0, 0))


def _matrix(rows, cols):
    return _resident((rows, cols), lambda *_: (0, 0))


def _cast_specs(layer, rows, cols, n_steps, step):
    assert rows % n_steps == 0
    rb = rows // n_steps
    return (pl.BlockSpec((None, rb, cols), lambda *ids: (layer, step(*ids), 0)),
            pl.BlockSpec((rb, cols), lambda *ids: (step(*ids), 0)))


def _full(shape):
    return pl.BlockSpec(shape, lambda *_: (0,) * len(shape))


def _drop_first_ref(body):
    def wrapped(_, *refs):
        return body(*refs)
    return wrapped


def _call_carrying(body, carried, carried_out, *, in_specs, args, **kwargs):
    if carried is None:
        return pl.pallas_call(body, in_specs=in_specs, **kwargs)(*args)
    return pl.pallas_call(
        _drop_first_ref(body), in_specs=[pl.BlockSpec(memory_space=pl.ANY)] + in_specs,
        input_output_aliases={0: carried_out}, **kwargs)(carried, *args)


def _rms(x, g):
    return x * lax.rsqrt(jnp.mean(x * x, axis=-1, keepdims=True) + EPS) * g


def _modulated_rms(x, g, scale, shift):
    return x * lax.rsqrt(jnp.mean(x * x, axis=-1, keepdims=True) + EPS) * (g * (1.0 + scale)) + shift


def _center_scale(x):
    mu = jnp.mean(x, axis=-1, keepdims=True)
    xc = x - mu
    var = jnp.mean(xc * xc, axis=-1, keepdims=True)
    return xc * lax.rsqrt(var + EPS)


def _dot(a, b):
    return jnp.dot(a, b, preferred_element_type=F32)


def _col(i):
    return slice(i * D_MODEL, (i + 1) * D_MODEL)


def _head(hd):
    return slice(hd * HEAD_DIM, (hd + 1) * HEAD_DIM)


def _ada_kernel(cp_ref, cs_ref, w_ref, b_ref, op_ref, os_ref):
    w = w_ref[...].astype(BF16)
    op_ref[...] = _dot(cp_ref[...].astype(BF16), w) + b_ref[...]
    os_ref[...] = _dot(cs_ref[...].astype(BF16), w) + b_ref[...]


def _ada_call(c_prompt, c_sample, w_ada, b_ada):
    depth, d, n = w_ada.shape
    bp, bs = c_prompt.shape[0], c_sample.shape[0]
    return pl.pallas_call(
        _ada_kernel,
        out_shape=(jax.ShapeDtypeStruct((depth, bp, n), F32),
                   jax.ShapeDtypeStruct((depth, bs, n), F32)),
        grid=(depth, n // ADA_BLOCK_N),
        in_specs=[
            _full((bp, d)),
            _full((bs, d)),
            pl.BlockSpec((None, d, ADA_BLOCK_N), lambda l, j: (l, 0, j)),
            pl.BlockSpec((None, 1, ADA_BLOCK_N), lambda l, j: (l, 0, j)),
        ],
        out_specs=(pl.BlockSpec((None, bp, ADA_BLOCK_N), lambda l, j: (l, 0, j)),
                   pl.BlockSpec((None, bs, ADA_BLOCK_N), lambda l, j: (l, 0, j))),
        compiler_params=pltpu.CompilerParams(
            dimension_semantics=("arbitrary", "arbitrary"), vmem_limit_bytes=VMEM_LIMIT_BYTES),
        name="ada",
    )(c_prompt, c_sample, w_ada, b_ada.reshape(depth, 1, n))


def _retention_one_token(c_ref, q_ref, k_ref, v_ref, s_ref, o_ref, sn_ref):
    n_rows = q_ref.shape[0]
    for hd in range(HEADS):
        q = q_ref[:, _head(hd)]
        k = k_ref[:, _head(hd)]
        v = v_ref[:, _head(hd)]
        decay, xi, zeta, g_l = c_ref[0, hd], c_ref[1, hd], c_ref[2, hd], c_ref[3, hd]
        sc = jnp.sum(q * k, axis=-1, keepdims=True) * decay
        pad = jnp.zeros((CHUNK - n_rows, HEAD_DIM), F32)
        q_t = jnp.concatenate([q, pad], axis=0).T
        kz_t = jnp.concatenate([k * zeta, pad], axis=0).T
        qs_rows = []
        for b in range(n_rows):
            state = s_ref[b, hd]
            qs_rows.append(jnp.sum(q_t[:, b:b + 1] * state, axis=0, keepdims=True))
            sn_ref[b, hd] = state * g_l + kz_t[:, b:b + 1] * v[b:b + 1, :]
        o_ref[:, _head(hd)] = sc * v + jnp.concatenate(qs_rows, axis=0) * xi


def _ffn_body(x, sh2, sc2, gt2, n2g, w1_ref, w2_ref, h_s):
    h_s[...] = _modulated_rms(x, n2g, sc2, sh2).astype(BF16)
    acc = None
    for j in range(D_FF // D_MODEL):
        hid = _dot(h_s[...], w1_ref[:, _col(j)])
        hid = jnp.square(jnp.maximum(hid, 0.0)).astype(BF16)
        part = _dot(hid, w2_ref[_col(j), :])
        acc = part if acc is None else acc + part
    return x + gt2 * acc


def _ffn_kernel(*refs, final):
    (c_ref, x_ref, mod_ref, n2g_ref, fg_ref, w1_ref, w2_ref, q_ref, k_ref, v_ref, s_ref) = refs[:11]
    h_s = refs[-1]
    if final:
        y_ref, o_ref, sn_ref = refs[11:14]
    else:
        win_ref, wout_ref, y_ref, o_ref, sn_ref, winb_ref, woutb_ref = refs[11:18]
        winb_ref[...] = win_ref[...].astype(BF16)
        woutb_ref[...] = wout_ref[...].astype(BF16)
    _retention_one_token(c_ref, q_ref, k_ref, v_ref, s_ref, o_ref, sn_ref)
    y = _ffn_body(x_ref[...], mod_ref[:, _col(3)], mod_ref[:, _col(4)], mod_ref[:, _col(5)],
                  n2g_ref[...], w1_ref, w2_ref, h_s)
    if final:
        y = _rms(y, fg_ref[...])
    y_ref[...] = y


def _ffn_call(layer, x, mod, n2g, fg, w1, w2, consts1, q, k, v, state_all, new_state_all, w_in, w_out,
              *, final):
    b, t, d = x.shape
    tm = TILE_M
    nj = t // tm
    n_steps, nb, _ = q.shape
    assert n_steps == b * nj
    step = lambda i, j: i * nj + j
    rows_spec = pl.BlockSpec((None, nb, d), lambda i, j: (step(i, j), 0, 0))
    state_spec = pl.BlockSpec((None, nb, HEADS, HEAD_DIM, HEAD_DIM),
                              lambda i, j: (layer, step(i, j), 0, 0, 0))
    in_specs = [
        pl.BlockSpec(memory_space=pltpu.SMEM),
        pl.BlockSpec((None, tm, d), lambda i, j: (i, j, 0)),
        pl.BlockSpec((None, None, 1, 6 * d), lambda i, j: (layer, i, 0, 0)),
        _layer_row(layer),
        _full((1, d)),
        _matrix(d, D_FF),
        _matrix(D_FF, d),
        rows_spec, rows_spec, rows_spec,
        state_spec,
    ]
    args = [consts1, x, mod, n2g, fg, w1, w2, q, k, v, state_all]
    out_shape = [jax.ShapeDtypeStruct(x.shape, F32),
                 jax.ShapeDtypeStruct(q.shape, F32),
                 jax.ShapeDtypeStruct(state_all.shape, F32)]
    out_specs = [pl.BlockSpec((None, tm, d), lambda i, j: (i, j, 0)), rows_spec, state_spec]
    if not final:
        for w in (w_in, w_out):
            src, dst = _cast_specs(layer + 1, w.shape[1], w.shape[2], n_steps, step)
            in_specs.append(src)
            args.append(w)
            out_specs.append(dst)
            out_shape.append(jax.ShapeDtypeStruct(w.shape[1:], BF16))
    return _call_carrying(
        functools.partial(_ffn_kernel, final=final), new_state_all, 2,
        out_shape=tuple(out_shape),
        grid=(b, nj),
        in_specs=in_specs,
        args=args,
        out_specs=tuple(out_specs),
        scratch_shapes=[pltpu.VMEM((tm, d), BF16)],
        compiler_params=pltpu.CompilerParams(
            dimension_semantics=("arbitrary", "arbitrary"), vmem_limit_bytes=VMEM_LIMIT_BYTES),
        name="ffn",
    )


def _mix_kernel(gl_ref, x_ref, mod_ref, cos_ref, sin_ref, n1g_ref, lng_ref, lnb_ref, gng_ref,
                ws_ref, bst_ref, decay_ref, xi_ref, zeta_ref, win_ref, wout_ref, wf1_ref, wf2_ref,
                xo_ref, s_ref, wf1b_ref, wf2b_ref,
                h_s, p_s, van_s, q_s, k_s, kz_s, v_s, a_s, on_s):
    n_chunks = x_ref.shape[0] // CHUNK

    @pl.when(pl.program_id(1) == 0)
    def _():
        s_ref[...] = jnp.zeros_like(s_ref)

    wf1b_ref[...] = wf1_ref[...].astype(BF16)
    wf2b_ref[...] = wf2_ref[...].astype(BF16)

    sh1 = mod_ref[:, _col(0)]
    sc1 = mod_ref[:, _col(1)]
    gt1 = mod_ref[:, _col(2)]
    h_s[...] = _modulated_rms(x_ref[...], n1g_ref[...], sc1, sh1).astype(BF16)

    def proj(slab):
        return _dot(h_s[...], win_ref[:, _col(slab)])

    def rows(c):
        return slice(c * CHUNK, (c + 1) * CHUNK)

    van_s[...] = (_center_scale(proj(SLAB_VA)) * lng_ref[...] + lnb_ref[...]).astype(BF16)
    p_s[...] = proj(SLAB_U)
    tril = (lax.broadcasted_iota(jnp.int32, (CHUNK, CHUNK), 0)
            >= lax.broadcasted_iota(jnp.int32, (CHUNK, CHUNK), 1))
    for g in range(GROUPS):
        w_g = jnp.where(tril, ws_ref[g], 0.0).astype(BF16)
        gc = slice(g * GROUP_DIM, (g + 1) * GROUP_DIM)
        for c in range(n_chunks):
            z = _dot(w_g, van_s[rows(c), gc]) + bst_ref[:, g:g + 1]
            a_s[rows(c), gc] = p_s[rows(c), gc] * z
    a_s[...] = jax.nn.sigmoid(proj(SLAB_GA)) * a_s[...]

    half = HEAD_DIM // 2

    def rotate(c, hd):
        lo = slice(hd * HEAD_DIM, hd * HEAD_DIM + half)
        hi = slice(hd * HEAD_DIM + half, (hd + 1) * HEAD_DIM)
        x1 = p_s[rows(c), lo]
        x2 = p_s[rows(c), hi]
        cos = cos_ref[rows(c), :]
        sin = sin_ref[rows(c), :]
        return lo, hi, x1 * cos - x2 * sin, x1 * sin + x2 * cos

    p_s[...] = proj(SLAB_Q)
    for c in range(n_chunks):
        for hd in range(HEADS):
            lo, hi, r1, r2 = rotate(c, hd)
            q_s[rows(c), lo] = r1.astype(BF16)
            q_s[rows(c), hi] = r2.astype(BF16)
    p_s[...] = proj(SLAB_K)
    k_scale = HEAD_DIM ** -0.5
    for c in range(n_chunks):
        for hd in range(HEADS):
            lo, hi, r1, r2 = rotate(c, hd)
            r1 = r1 * k_scale
            r2 = r2 * k_scale
            k_s[rows(c), lo] = r1.astype(BF16)
            k_s[rows(c), hi] = r2.astype(BF16)
            zr = slice(c * CHUNK % RET_CHUNK, c * CHUNK % RET_CHUNK + CHUNK)
            kz_s[rows(c), lo] = (r1 * zeta_ref[zr, lo]).astype(BF16)
            kz_s[rows(c), hi] = (r2 * zeta_ref[zr, hi]).astype(BF16)
    v_s[...] = proj(SLAB_VR).astype(BF16)

    for c in range(x_ref.shape[0] // RET_CHUNK):
        rc = slice(c * RET_CHUNK, (c + 1) * RET_CHUNK)
        for hd in range(HEADS):
            hc = _head(hd)
            qc = q_s[rc, hc]
            vc = v_s[rc, hc]
            state = s_ref[hd]
            sc = lax.dot_general(qc, k_s[rc, hc], (((1,), (1,)), ((), ())),
                                 preferred_element_type=F32) * decay_ref[hd]
            o = _dot(sc.astype(BF16), vc) + _dot(qc, state.astype(BF16)) * xi_ref[:, hc]
            s_ref[hd] = state * gl_ref[hd] + lax.dot_general(
                kz_s[rc, hc], vc, (((0,), (0,)), ((), ())), preferred_element_type=F32)
            on_s[rc, hc] = _center_scale(o) * gng_ref[:, hc]

    on_s[...] = jax.nn.silu(proj(SLAB_G)) * on_s[...]
    h_s[...] = (a_s[...] + jax.nn.sigmoid(proj(SLAB_GB)) * on_s[...]).astype(BF16)
    xo_ref[...] = x_ref[...] + gt1 * _dot(h_s[...], wout_ref[...])


def _mix_call(layer, x, mod, cos, sin, n1g, lng, lnb, gng, ws, bst, decay, xi, zeta, gl, w_in, w_out,
              w_ff1, w_ff2, ret_all):
    b, t, d = x.shape
    depth = w_ff1.shape[0]
    tm = TILE_M
    nj = t // tm
    step = lambda i, j: i * nj + j
    ff1_src, ff1_dst = _cast_specs(layer, d, D_FF, b * nj, step)
    ff2_src, ff2_dst = _cast_specs(layer, D_FF, d, b * nj, step)
    return _call_carrying(
        _mix_kernel, ret_all, 1,
        out_shape=(jax.ShapeDtypeStruct(x.shape, F32),
                   jax.ShapeDtypeStruct((depth, b, HEADS, HEAD_DIM, HEAD_DIM), F32),
                   jax.ShapeDtypeStruct((d, D_FF), BF16),
                   jax.ShapeDtypeStruct((D_FF, d), BF16)),
        grid=(b, nj),
        in_specs=[
            pl.BlockSpec(memory_space=pltpu.SMEM),
            pl.BlockSpec((None, tm, d), lambda i, j: (i, j, 0)),
            pl.BlockSpec((None, None, 1, 6 * d), lambda i, j: (layer, i, 0, 0)),
            pl.BlockSpec((tm, HEAD_DIM // 2), lambda i, j: (j, 0)),
            pl.BlockSpec((tm, HEAD_DIM // 2), lambda i, j: (j, 0)),
            _layer_row(layer), _layer_row(layer), _layer_row(layer), _layer_row(layer),
            _resident((None, GROUPS, CHUNK, CHUNK), lambda i, j: (layer, 0, 0, 0)),
            _layer_matrix(layer, CHUNK, GROUPS),
            _resident((HEADS, RET_CHUNK, RET_CHUNK), lambda i, j: (0, 0, 0)),
            _resident((RET_CHUNK, d), lambda i, j: (0, 0)),
            _resident((RET_CHUNK, d), lambda i, j: (0, 0)),
            _matrix(d, N_SLABS * d),
            _matrix(d, d),
            ff1_src, ff2_src,
        ],
        args=[gl, x, mod, cos, sin, n1g, lng, lnb, gng, ws, bst, decay, xi, zeta, w_in, w_out,
              w_ff1, w_ff2],
        out_specs=(
            pl.BlockSpec((None, tm, d), lambda i, j: (i, j, 0)),
            pl.BlockSpec((None, None, HEADS, HEAD_DIM, HEAD_DIM), lambda i, j: (layer, i, 0, 0, 0)),
            ff1_dst, ff2_dst,
        ),
        scratch_shapes=[
            pltpu.VMEM((tm, d), BF16),
            pltpu.VMEM((tm, d), F32),
            pltpu.VMEM((tm, d), BF16),
            pltpu.VMEM((tm, d), BF16),
            pltpu.VMEM((tm, d), BF16),
            pltpu.VMEM((tm, d), BF16),
            pltpu.VMEM((tm, d), BF16),
            pltpu.VMEM((tm, d), F32),
            pltpu.VMEM((tm, d), F32),
        ],
        compiler_params=pltpu.CompilerParams(
            dimension_semantics=("arbitrary", "arbitrary"), vmem_limit_bytes=VMEM_LIMIT_BYTES),
        name="mix",
    )


def _sproj_kernel(x_ref, mod_ref, cos_ref, sin_ref, n1g_ref, lng_ref, lnb_ref, w0_ref, b0_ref,
                  win_ref, van_ref, ma_ref, q_ref, k_ref, v_ref, g_ref, gb_ref, h_s):
    sh1 = mod_ref[:, _col(0)]
    sc1 = mod_ref[:, _col(1)]
    h_s[...] = _modulated_rms(x_ref[...], n1g_ref[...], sc1, sh1).astype(BF16)

    def proj(slab):
        return _dot(h_s[...], win_ref[:, _col(slab)])

    van = _center_scale(proj(SLAB_VA)) * lng_ref[...] + lnb_ref[...]
    van_ref[...] = van
    z = van * w0_ref[...] + b0_ref[...]
    ma_ref[...] = jax.nn.sigmoid(proj(SLAB_GA)) * (proj(SLAB_U) * z)

    half = HEAD_DIM // 2
    cos = cos_ref[...]
    sin = sin_ref[...]

    def rotary_to(dst_ref, p, scale):
        for hd in range(HEADS):
            lo = slice(hd * HEAD_DIM, hd * HEAD_DIM + half)
            hi = slice(hd * HEAD_DIM + half, (hd + 1) * HEAD_DIM)
            x1 = p[:, lo]
            x2 = p[:, hi]
            dst_ref[:, lo] = (x1 * cos - x2 * sin) * scale
            dst_ref[:, hi] = (x1 * sin + x2 * cos) * scale

    rotary_to(q_ref, proj(SLAB_Q), 1.0)
    rotary_to(k_ref, proj(SLAB_K), HEAD_DIM ** -0.5)
    v_ref[...] = proj(SLAB_VR)
    g_ref[...] = proj(SLAB_G)
    gb_ref[...] = proj(SLAB_GB)


def _sproj_call(layer, x, mod, cos, sin, n1g, lng, lnb, w0, b0, w_in, van_all):
    n, d = x.shape
    depth = mod.shape[0]
    out = jax.ShapeDtypeStruct((n, d), F32)
    return _call_carrying(
        _sproj_kernel, van_all, 0,
        out_shape=(jax.ShapeDtypeStruct((depth, n, d), F32),) + (out,) * 6,
        grid=(1,),
        in_specs=[
            _full((n, d)),
            pl.BlockSpec((None, n, 6 * d), lambda i: (layer, 0, 0)),
            _full((1, HEAD_DIM // 2)), _full((1, HEAD_DIM // 2)),
            _layer_row(layer), _layer_row(layer), _layer_row(layer), _layer_row(layer),
            _layer_row(layer),
            _matrix(d, N_SLABS * d),
        ],
        args=[x, mod, cos, sin, n1g, lng, lnb, w0, b0, w_in],
        out_specs=(pl.BlockSpec((None, n, d), lambda i: (layer, 0, 0)),) + (_full((n, d)),) * 6,
        scratch_shapes=[pltpu.VMEM((n, d), BF16)],
        compiler_params=pltpu.CompilerParams(
            dimension_semantics=("arbitrary",), vmem_limit_bytes=VMEM_LIMIT_BYTES),
        name="sproj",
    )


def _spost_kernel(o_ref, g_ref, gb_ref, ma_ref, x_ref, mod_ref, gng_ref, n2g_ref, fg_ref,
                  wout_ref, w1_ref, w2_ref, y_ref, h_s, *, final):
    for hd in range(HEADS):
        hc = _head(hd)
        r = jax.nn.silu(g_ref[:, hc]) * (_center_scale(o_ref[:, hc]) * gng_ref[:, hc])
        h_s[:, hc] = (ma_ref[:, hc] + jax.nn.sigmoid(gb_ref[:, hc]) * r).astype(BF16)
    x1 = x_ref[...] + mod_ref[:, _col(2)] * _dot(h_s[...], wout_ref[...])
    y = _ffn_body(x1, mod_ref[:, _col(3)], mod_ref[:, _col(4)], mod_ref[:, _col(5)],
                  n2g_ref[...], w1_ref, w2_ref, h_s)
    if final:
        y = _rms(y, fg_ref[...])
    y_ref[...] = y


def _spost_call(layer, o, g, gb, ma, x, mod, gng, n2g, fg, w_out, w1, w2, *, final):
    n, d = x.shape
    return pl.pallas_call(
        functools.partial(_spost_kernel, final=final),
        out_shape=jax.ShapeDtypeStruct((n, d), F32),
        grid=(1,),
        in_specs=[
            _full((n, d)), _full((n, d)), _full((n, d)), _full((n, d)), _full((n, d)),
            pl.BlockSpec((None, n, 6 * d), lambda i: (layer, 0, 0)),
            _layer_row(layer), _layer_row(layer),
            _full((1, d)),
            _matrix(d, d),
            _matrix(d, D_FF),
            _matrix(D_FF, d),
        ],
        out_specs=_full((n, d)),
        scratch_shapes=[pltpu.VMEM((n, d), BF16)],
        compiler_params=pltpu.CompilerParams(
            dimension_semantics=("arbitrary",), vmem_limit_bytes=VMEM_LIMIT_BYTES),
        name="spost",
    )(o, g, gb, ma, x, mod, gng, n2g, fg, w_out, w1, w2)


def _rope_tables(pos0, t):
    half = HEAD_DIM // 2
    inv = ROPE_BASE ** (-jnp.arange(half, dtype=F32) / half)
    pos = pos0 + jnp.arange(t, dtype=F32)
    ang = pos[:, None] * inv[None, :]
    return jnp.cos(ang), jnp.sin(ang)


def _retention_tables(length):
    log_g = jnp.log1p(-jnp.exp2(-5.0 - jnp.arange(HEADS, dtype=F32)))
    idx = jnp.arange(length, dtype=F32)
    diff = idx[:, None] - idx[None, :]
    decay = jnp.where(diff[None] >= 0.0,
                      jnp.exp(jnp.maximum(diff, 0.0)[None] * log_g[:, None, None]), 0.0)
    xi = jnp.exp((idx[:, None] + 1.0) * log_g[None, :])
    zeta = jnp.exp((length - 1.0 - idx)[:, None] * log_g[None, :])
    g_len = jnp.exp(length * log_g)
    return decay, xi, zeta, g_len


def kernel(x_prompt, x_sample, state_ret, c_prompt, c_sample, w_ada, b_ada, norm1_g, w_in, ln_v_g,
           ln_v_b, w_s, b_s, gn_g, w_out, norm2_g, w_ff1, w_ff2, final_g):
    depth = w_in.shape[0]
    batch, seq, d = x_prompt.shape
    n_sample = x_sample.shape[0]
    n_steps = batch * (seq // TILE_M)
    assert d == D_MODEL and seq % TILE_M == 0 and x_sample.shape[1] == 1
    assert TILE_M % RET_CHUNK == 0 and RET_CHUNK % CHUNK == 0
    assert n_sample % n_steps == 0

    w_in_b = w_in[0].astype(BF16)
    w_out_b = w_out[0].astype(BF16)
    row = lambda p: p.reshape(depth, 1, d)
    n1g, lng, lnb, gng, n2g = row(norm1_g), row(ln_v_g), row(ln_v_b), row(gn_g), row(norm2_g)
    fg = final_g.reshape(1, d)

    mod_p, mod_s = _ada_call(c_prompt, c_sample, w_ada, b_ada)
    mod_p = mod_p.reshape(depth, batch, 1, 6 * d)

    cos_p, sin_p = _rope_tables(0.0, seq)
    cos_s, sin_s = _rope_tables(float(PAST_LEN), 1)
    decay, xi, zeta, g_len = _retention_tables(RET_CHUNK)
    xi_cols = jnp.repeat(xi, HEAD_DIM, axis=1)
    zeta_cols = jnp.repeat(zeta, HEAD_DIM, axis=1)
    decay1, xi1, zeta1, g_len1 = _retention_tables(1)
    consts1 = jnp.stack([decay1[:, 0, 0], xi1[0], zeta1[0], g_len1])
    b_s_t = jnp.swapaxes(b_s, 1, 2)
    w0_cols = jnp.repeat(w_s[:, :, 0, 0], GROUP_DIM, axis=1).reshape(depth, 1, d)
    b0_cols = jnp.repeat(b_s[:, :, 0], GROUP_DIM, axis=1).reshape(depth, 1, d)

    xp = x_prompt
    xs = x_sample.reshape(n_sample, d)
    ret_prompt = new_state = van_all = None
    split = lambda a: a.reshape(n_steps, n_sample // n_steps, d)
    for l in range(depth):
        final = l == depth - 1
        xp, ret_prompt, w_ff1_b, w_ff2_b = _mix_call(
            l, xp, mod_p, cos_p, sin_p, n1g, lng, lnb, gng, w_s, b_s_t, decay, xi_cols, zeta_cols,
            g_len, w_in_b, w_out_b, w_ff1, w_ff2, ret_prompt)
        van_all, ma, q, k, v, g, gb = _sproj_call(l, xs, mod_s, cos_s, sin_s, n1g, lng, lnb, w0_cols,
                                                  b0_cols, w_in_b, van_all)
        xp, o, new_state, *next_w = _ffn_call(
            l, xp, mod_p, n2g, fg, w_ff1_b, w_ff2_b, consts1, split(q), split(k), split(v),
            state_ret, new_state, w_in, w_out, final=final)
        xs = _spost_call(l, o.reshape(n_sample, d), g, gb, ma, xs, mod_s, gng, n2g, fg, w_out_b,
                         w_ff1_b, w_ff2_b, final=final)
        if not final:
            w_in_b, w_out_b = next_w

    return (xp, xs.reshape(x_sample.shape), ret_prompt, new_state,
            van_all.reshape(depth, n_sample, 1, d))
```

```python
import functools

import jax
import jax.numpy as jnp
from jax import lax
from jax.experimental import pallas as pl
from jax.experimental.pallas import tpu as pltpu

F32 = jnp.float32
BF16 = jnp.bfloat16

D_MODEL = 1024
CHUNK = 128
GROUPS = 4
GROUP_DIM = D_MODEL // GROUPS
HEADS = 4
HEAD_DIM = D_MODEL // HEADS
D_FF = 4 * D_MODEL
N_SLABS = 8
SLAB_U, SLAB_VA, SLAB_Q, SLAB_K, SLAB_VR, SLAB_G, SLAB_GA, SLAB_GB = range(N_SLABS)
PAST_LEN = 16384
ROPE_BASE = 10000.0
EPS = 1e-6
SUBLANES = 8

RET_CHUNK = 256
TILE_M = 512
ADA_BLOCK_N = 1536
VMEM_LIMIT_BYTES = 56 * 1024 * 1024


def _resident(block_shape, index_map):
    return pl.BlockSpec(block_shape, index_map, pipeline_mode=pl.Buffered(1))


def _layer_matrix(layer, rows, cols):
    return _resident((None, rows, cols), lambda *_: (layer, 0, 0))


def _matrix(rows, cols):
    return _resident((rows, cols), lambda *_: (0, 0))


def _cast_specs(layer, rows, cols, n_steps, step):
    assert rows % n_steps == 0
    rb = rows // n_steps
    return (pl.BlockSpec((None, rb, cols), lambda *ids: (layer, step(*ids), 0)),
            pl.BlockSpec((rb, cols), lambda *ids: (step(*ids), 0)))


def _full(shape):
    return pl.BlockSpec(shape, lambda *_: (0,) * len(shape))


def _drop_first_ref(body):
    def wrapped(_, *refs):
        return body(*refs)
    return wrapped


def _call_carrying(body, carried, carried_out, *, in_specs, args, **kwargs):
    if carried is None:
        return pl.pallas_call(body, in_specs=in_specs, **kwargs)(*args)
    return pl.pallas_call(
        _drop_first_ref(body), in_specs=[pl.BlockSpec(memory_space=pl.ANY)] + in_specs,
        input_output_aliases={0: carried_out}, **kwargs)(carried, *args)


def _rms(x, g):
    return x * lax.rsqrt(jnp.mean(x * x, axis=-1, keepdims=True) + EPS) * g


def _modulated_rms(x, g, scale, shift):
    return x * lax.rsqrt(jnp.mean(x * x, axis=-1, keepdims=True) + EPS) * (g * (1.0 + scale)) + shift


def _center_scale(x):
    mu = jnp.mean(x, axis=-1, keepdims=True)
    xc = x - mu
    var = jnp.mean(xc * xc, axis=-1, keepdims=True)
    return xc * lax.rsqrt(var + EPS)


def _dot(a, b):
    return jnp.dot(a, b, preferred_element_type=F32)


def _col(i):
    return slice(i * D_MODEL, (i + 1) * D_MODEL)


def _head(hd):
    return slice(hd * HEAD_DIM, (hd + 1) * HEAD_DIM)


def _ada_kernel(cp_ref, cs_ref, w_ref, b_ref, op_ref, os_ref):
    w = w_ref[...].astype(BF16)
    op_ref[...] = _dot(cp_ref[...].astype(BF16), w) + b_ref[...]
    os_ref[...] = _dot(cs_ref[...].astype(BF16), w) + b_ref[...]


def _ada_call(c_prompt, c_sample, w_ada, b_ada):
    depth, d, n = w_ada.shape
    bp, bs = c_prompt.shape[0], c_sample.shape[0]
    return pl.pallas_call(
        _ada_kernel,
        out_shape=(jax.ShapeDtypeStruct((depth, bp, n), F32),
                   jax.ShapeDtypeStruct((depth, bs, n), F32)),
        grid=(depth, n // ADA_BLOCK_N),
        in_specs=[
            _full((bp, d)),
            _full((bs, d)),
            pl.BlockSpec((None, d, ADA_BLOCK_N), lambda l, j: (l, 0, j)),
            pl.BlockSpec((None, 1, ADA_BLOCK_N), lambda l, j: (l, 0, j)),
        ],
        out_specs=(pl.BlockSpec((None, bp, ADA_BLOCK_N), lambda l, j: (l, 0, j)),
                   pl.BlockSpec((None, bs, ADA_BLOCK_N), lambda l, j: (l, 0, j))),
        compiler_params=pltpu.CompilerParams(
            dimension_semantics=("arbitrary", "arbitrary"), vmem_limit_bytes=VMEM_LIMIT_BYTES),
        name="ada",
    )(c_prompt, c_sample, w_ada, b_ada.reshape(depth, 1, n))


def _retention_one_token(c_ref, q_ref, k_ref, v_ref, s_ref, o_ref, sn_ref, row):
    n_rows = q_ref.shape[0]

    def rows_of(ref, hd):
        return jnp.concatenate([ref[a, pl.ds(row, 1), _head(hd)] for a in range(n_rows)], axis=0)

    for hd in range(HEADS):
        q = rows_of(q_ref, hd)
        k = rows_of(k_ref, hd)
        v = rows_of(v_ref, hd)
        decay, xi, zeta, g_l = c_ref[0, hd], c_ref[1, hd], c_ref[2, hd], c_ref[3, hd]
        sc = jnp.sum(q * k, axis=-1, keepdims=True) * decay
        pad = jnp.zeros((CHUNK - n_rows, HEAD_DIM), F32)
        q_t = jnp.concatenate([q, pad], axis=0).T
        kz_t = jnp.concatenate([k * zeta, pad], axis=0).T
        qs_rows = []
        for b in range(n_rows):
            state = s_ref[b, hd]
            qs_rows.append(jnp.sum(q_t[:, b:b + 1] * state, axis=0, keepdims=True))
            sn_ref[b, hd] = state * g_l + kz_t[:, b:b + 1] * v[b:b + 1, :]
        o = sc * v + jnp.concatenate(qs_rows, axis=0) * xi
        for a in range(n_rows):
            o_ref[a, pl.ds(row, 1), _head(hd)] = o[a:a + 1, :]


def _ffn_body(x, sh2, sc2, gt2, n2g, w1_ref, w2_ref, h_s):
    h_s[...] = _modulated_rms(x, n2g, sc2, sh2).astype(BF16)
    acc = None
    for j in range(D_FF // D_MODEL):
        hid = _dot(h_s[...], w1_ref[:, _col(j)])
        hid = jnp.square(jnp.maximum(hid, 0.0)).astype(BF16)
        part = _dot(hid, w2_ref[_col(j), :])
        acc = part if acc is None else acc + part
    return x + gt2 * acc


def _ffn_kernel(*refs, layer, final):
    (c_ref, x_ref, mod_ref, n2g_ref, fg_ref, w1_ref, w2_ref, q_ref, k_ref, v_ref, s_ref) = refs[:11]
    h_s = refs[-1]
    if final:
        y_ref, o_ref, sn_ref = refs[11:14]
    else:
        win_ref, wout_ref, y_ref, o_ref, sn_ref, winb_ref, woutb_ref = refs[11:18]
        winb_ref[...] = win_ref[...].astype(BF16)
        woutb_ref[...] = wout_ref[...].astype(BF16)
    step = pl.program_id(0) * pl.num_programs(1) + pl.program_id(1)
    _retention_one_token(c_ref, q_ref, k_ref, v_ref, s_ref, o_ref, sn_ref, step % SUBLANES)
    mod = mod_ref.at[pl.ds(pl.program_id(0), 1)]
    y = _ffn_body(x_ref[...], mod[:, _col(3)], mod[:, _col(4)], mod[:, _col(5)],
                  n2g_ref[layer:layer + 1, :], w1_ref, w2_ref, h_s)
    if final:
        y = _rms(y, fg_ref[...])
    y_ref[...] = y


def _ffn_call(layer, x, mod, n2g, fg, w1, w2, consts1, q, k, v, state_all, new_state_all, w_in, w_out,
              *, final):
    b, t, d = x.shape
    depth = mod.shape[0]
    tm = TILE_M
    nj = t // tm
    nb, n_steps, _ = q.shape
    assert n_steps == b * nj and n_steps % SUBLANES == 0
    step = lambda i, j: i * nj + j
    rows_spec = pl.BlockSpec((nb, SUBLANES, d), lambda i, j: (0, step(i, j) // SUBLANES, 0))
    state_spec = pl.BlockSpec((None, nb, None, HEADS, HEAD_DIM, HEAD_DIM),
                              lambda i, j: (layer, 0, step(i, j), 0, 0, 0))
    in_specs = [
        pl.BlockSpec(memory_space=pltpu.SMEM),
        pl.BlockSpec((None, tm, d), lambda i, j: (i, j, 0)),
        _layer_matrix(layer, b, 6 * d),
        _resident((depth, d), lambda i, j: (0, 0)),
        _full((1, d)),
        _matrix(d, D_FF),
        _matrix(D_FF, d),
        rows_spec, rows_spec, rows_spec,
        state_spec,
    ]
    args = [consts1, x, mod, n2g, fg, w1, w2, q, k, v, state_all]
    out_shape = [jax.ShapeDtypeStruct(x.shape, F32),
                 jax.ShapeDtypeStruct(q.shape, F32),
                 jax.ShapeDtypeStruct(state_all.shape, F32)]
    out_specs = [pl.BlockSpec((None, tm, d), lambda i, j: (i, j, 0)), rows_spec, state_spec]
    if not final:
        for w in (w_in, w_out):
            src, dst = _cast_specs(layer + 1, w.shape[1], w.shape[2], n_steps, step)
            in_specs.append(src)
            args.append(w)
            out_specs.append(dst)
            out_shape.append(jax.ShapeDtypeStruct(w.shape[1:], BF16))
    return _call_carrying(
        functools.partial(_ffn_kernel, layer=layer, final=final), new_state_all, 2,
        out_shape=tuple(out_shape),
        grid=(b, nj),
        in_specs=in_specs,
        args=args,
        out_specs=tuple(out_specs),
        scratch_shapes=[pltpu.VMEM((tm, d), BF16)],
        compiler_params=pltpu.CompilerParams(
            dimension_semantics=("arbitrary", "arbitrary"), vmem_limit_bytes=VMEM_LIMIT_BYTES),
        name="ffn",
    )


def _mix_kernel(gl_ref, x_ref, mod_ref, cos_ref, sin_ref, n1g_ref, lng_ref, lnb_ref, gng_ref,
                ws_ref, bst_ref, decay_ref, xi_ref, zeta_ref, win_ref, wout_ref, wf1_ref, wf2_ref,
                xo_ref, s_ref, wf1b_ref, wf2b_ref,
                h_s, p_s, van_s, q_s, k_s, kz_s, v_s, a_s, on_s, *, layer):
    n_chunks = x_ref.shape[0] // CHUNK
    this = slice(layer, layer + 1)

    @pl.when(pl.program_id(1) == 0)
    def _():
        s_ref[...] = jnp.zeros_like(s_ref)

    wf1b_ref[...] = wf1_ref[...].astype(BF16)
    wf2b_ref[...] = wf2_ref[...].astype(BF16)

    mod = mod_ref.at[pl.ds(pl.program_id(0), 1)]
    sh1 = mod[:, _col(0)]
    sc1 = mod[:, _col(1)]
    gt1 = mod[:, _col(2)]
    h_s[...] = _modulated_rms(x_ref[...], n1g_ref[this, :], sc1, sh1).astype(BF16)

    def proj(slab):
        return _dot(h_s[...], win_ref[:, _col(slab)])

    def rows(c):
        return slice(c * CHUNK, (c + 1) * CHUNK)

    van_s[...] = (_center_scale(proj(SLAB_VA)) * lng_ref[this, :] + lnb_ref[this, :]).astype(BF16)
    p_s[...] = proj(SLAB_U)
    tril = (lax.broadcasted_iota(jnp.int32, (CHUNK, CHUNK), 0)
            >= lax.broadcasted_iota(jnp.int32, (CHUNK, CHUNK), 1))
    for g in range(GROUPS):
        w_g = jnp.where(tril, ws_ref[g], 0.0).astype(BF16)
        gc = slice(g * GROUP_DIM, (g + 1) * GROUP_DIM)
        for c in range(n_chunks):
            z = _dot(w_g, van_s[rows(c), gc]) + bst_ref[:, g:g + 1]
            a_s[rows(c), gc] = p_s[rows(c), gc] * z
    a_s[...] = jax.nn.sigmoid(proj(SLAB_GA)) * a_s[...]

    half = HEAD_DIM // 2

    def rotate(c, hd):
        lo = slice(hd * HEAD_DIM, hd * HEAD_DIM + half)
        hi = slice(hd * HEAD_DIM + half, (hd + 1) * HEAD_DIM)
        x1 = p_s[rows(c), lo]
        x2 = p_s[rows(c), hi]
        cos = cos_ref[rows(c), :]
        sin = sin_ref[rows(c), :]
        return lo, hi, x1 * cos - x2 * sin, x1 * sin + x2 * cos

    p_s[...] = proj(SLAB_Q)
    for c in range(n_chunks):
        for hd in range(HEADS):
            lo, hi, r1, r2 = rotate(c, hd)
            q_s[rows(c), lo] = r1.astype(BF16)
            q_s[rows(c), hi] = r2.astype(BF16)
    p_s[...] = proj(SLAB_K)
    k_scale = HEAD_DIM ** -0.5
    for c in range(n_chunks):
        for hd in range(HEADS):
            lo, hi, r1, r2 = rotate(c, hd)
            r1 = r1 * k_scale
            r2 = r2 * k_scale
            k_s[rows(c), lo] = r1.astype(BF16)
            k_s[rows(c), hi] = r2.astype(BF16)
            zr = slice(c * CHUNK % RET_CHUNK, c * CHUNK % RET_CHUNK + CHUNK)
            kz_s[rows(c), lo] = (r1 * zeta_ref[zr, lo]).astype(BF16)
            kz_s[rows(c), hi] = (r2 * zeta_ref[zr, hi]).astype(BF16)
    v_s[...] = proj(SLAB_VR).astype(BF16)

    for c in range(x_ref.shape[0] // RET_CHUNK):
        rc = slice(c * RET_CHUNK, (c + 1) * RET_CHUNK)
        for hd in range(HEADS):
            hc = _head(hd)
            qc = q_s[rc, hc]
            vc = v_s[rc, hc]
            state = s_ref[hd]
            sc = lax.dot_general(qc, k_s[rc, hc], (((1,), (1,)), ((), ())),
                                 preferred_element_type=F32) * decay_ref[hd]
            o = _dot(sc.astype(BF16), vc) + _dot(qc, state.astype(BF16)) * xi_ref[:, hc]
            s_ref[hd] = state * gl_ref[hd] + lax.dot_general(
                kz_s[rc, hc], vc, (((0,), (0,)), ((), ())), preferred_element_type=F32)
            on_s[rc, hc] = _center_scale(o) * gng_ref[this, hc]

    on_s[...] = jax.nn.silu(proj(SLAB_G)) * on_s[...]
    h_s[...] = (a_s[...] + jax.nn.sigmoid(proj(SLAB_GB)) * on_s[...]).astype(BF16)
    xo_ref[...] = x_ref[...] + gt1 * _dot(h_s[...], wout_ref[...])


def _mix_call(layer, x, mod, cos, sin, n1g, lng, lnb, gng, ws, bst, decay, xi, zeta, gl, w_in, w_out,
              w_ff1, w_ff2, ret_all):
    b, t, d = x.shape
    depth = w_ff1.shape[0]
    tm = TILE_M
    nj = t // tm
    step = lambda i, j: i * nj + j
    ff1_src, ff1_dst = _cast_specs(layer, d, D_FF, b * nj, step)
    ff2_src, ff2_dst = _cast_specs(layer, D_FF, d, b * nj, step)
    params = _resident((depth, d), lambda i, j: (0, 0))
    return _call_carrying(
        functools.partial(_mix_kernel, layer=layer), ret_all, 1,
        out_shape=(jax.ShapeDtypeStruct(x.shape, F32),
                   jax.ShapeDtypeStruct((depth, b, HEADS, HEAD_DIM, HEAD_DIM), F32),
                   jax.ShapeDtypeStruct((d, D_FF), BF16),
                   jax.ShapeDtypeStruct((D_FF, d), BF16)),
        grid=(b, nj),
        in_specs=[
            pl.BlockSpec(memory_space=pltpu.SMEM),
            pl.BlockSpec((None, tm, d), lambda i, j: (i, j, 0)),
            _layer_matrix(layer, b, 6 * d),
            pl.BlockSpec((tm, HEAD_DIM // 2), lambda i, j: (j, 0)),
            pl.BlockSpec((tm, HEAD_DIM // 2), lambda i, j: (j, 0)),
            params, params, params, params,
            _resident((None, GROUPS, CHUNK, CHUNK), lambda i, j: (layer, 0, 0, 0)),
            _layer_matrix(layer, CHUNK, GROUPS),
            _resident((HEADS, RET_CHUNK, RET_CHUNK), lambda i, j: (0, 0, 0)),
            _resident((RET_CHUNK, d), lambda i, j: (0, 0)),
            _resident((RET_CHUNK, d), lambda i, j: (0, 0)),
            _matrix(d, N_SLABS * d),
            _matrix(d, d),
            ff1_src, ff2_src,
        ],
        args=[gl, x, mod, cos, sin, n1g, lng, lnb, gng, ws, bst, decay, xi, zeta, w_in, w_out,
              w_ff1, w_ff2],
        out_specs=(
            pl.BlockSpec((None, tm, d), lambda i, j: (i, j, 0)),
            pl.BlockSpec((None, None, HEADS, HEAD_DIM, HEAD_DIM), lambda i, j: (layer, i, 0, 0, 0)),
            ff1_dst, ff2_dst,
        ),
        scratch_shapes=[
            pltpu.VMEM((tm, d), BF16),
            pltpu.VMEM((tm, d), F32),
            pltpu.VMEM((tm, d), BF16),
            pltpu.VMEM((tm, d), BF16),
            pltpu.VMEM((tm, d), BF16),
            pltpu.VMEM((tm, d), BF16),
            pltpu.VMEM((tm, d), BF16),
            pltpu.VMEM((tm, d), F32),
            pltpu.VMEM((tm, d), F32),
        ],
        compiler_params=pltpu.CompilerParams(
            dimension_semantics=("arbitrary", "arbitrary"), vmem_limit_bytes=VMEM_LIMIT_BYTES),
        name="mix",
    )


def _sproj_kernel(x_ref, mod_ref, cos_ref, sin_ref, n1g_ref, lng_ref, lnb_ref, w0_ref, b0_ref,
                  win_ref, van_ref, ma_ref, q_ref, k_ref, v_ref, g_ref, gb_ref, h_s, *, layer):
    this = slice(layer, layer + 1)
    sh1 = mod_ref[:, _col(0)]
    sc1 = mod_ref[:, _col(1)]
    h_s[...] = _modulated_rms(x_ref[...], n1g_ref[this, :], sc1, sh1).astype(BF16)

    def proj(slab):
        return _dot(h_s[...], win_ref[:, _col(slab)])

    van = _center_scale(proj(SLAB_VA)) * lng_ref[this, :] + lnb_ref[this, :]
    van_ref[...] = van
    z = van * w0_ref[this, :] + b0_ref[this, :]
    ma_ref[...] = jax.nn.sigmoid(proj(SLAB_GA)) * (proj(SLAB_U) * z)

    half = HEAD_DIM // 2
    cos = cos_ref[...]
    sin = sin_ref[...]

    def rotary_to(dst_ref, p, scale):
        for hd in range(HEADS):
            lo = slice(hd * HEAD_DIM, hd * HEAD_DIM + half)
            hi = slice(hd * HEAD_DIM + half, (hd + 1) * HEAD_DIM)
            x1 = p[:, lo]
            x2 = p[:, hi]
            dst_ref[:, lo] = (x1 * cos - x2 * sin) * scale
            dst_ref[:, hi] = (x1 * sin + x2 * cos) * scale

    rotary_to(q_ref, proj(SLAB_Q), 1.0)
    rotary_to(k_ref, proj(SLAB_K), HEAD_DIM ** -0.5)
    v_ref[...] = proj(SLAB_VR)
    g_ref[...] = proj(SLAB_G)
    gb_ref[...] = proj(SLAB_GB)


def _sproj_call(layer, x, mod, cos, sin, n1g, lng, lnb, w0, b0, w_in, van_all):
    n, d = x.shape
    depth = mod.shape[0]
    out = jax.ShapeDtypeStruct((n, d), F32)
    params = _full((depth, d))
    return _call_carrying(
        functools.partial(_sproj_kernel, layer=layer), van_all, 0,
        out_shape=(jax.ShapeDtypeStruct((depth, n, d), F32),) + (out,) * 6,
        grid=(1,),
        in_specs=[
            _full((n, d)),
            pl.BlockSpec((None, n, 6 * d), lambda i: (layer, 0, 0)),
            _full((1, HEAD_DIM // 2)), _full((1, HEAD_DIM // 2)),
            params, params, params, params, params,
            _matrix(d, N_SLABS * d),
        ],
        args=[x, mod, cos, sin, n1g, lng, lnb, w0, b0, w_in],
        out_specs=(pl.BlockSpec((None, n, d), lambda i: (layer, 0, 0)),) + (_full((n, d)),) * 6,
        scratch_shapes=[pltpu.VMEM((n, d), BF16)],
        compiler_params=pltpu.CompilerParams(
            dimension_semantics=("arbitrary",), vmem_limit_bytes=VMEM_LIMIT_BYTES),
        name="sproj",
    )


def _spost_kernel(o_ref, g_ref, gb_ref, ma_ref, x_ref, mod_ref, gng_ref, n2g_ref, fg_ref,
                  wout_ref, w1_ref, w2_ref, y_ref, h_s, *, layer, final):
    this = slice(layer, layer + 1)
    for hd in range(HEADS):
        hc = _head(hd)
        r = jax.nn.silu(g_ref[:, hc]) * (_center_scale(o_ref[:, hc]) * gng_ref[this, hc])
        h_s[:, hc] = (ma_ref[:, hc] + jax.nn.sigmoid(gb_ref[:, hc]) * r).astype(BF16)
    x1 = x_ref[...] + mod_ref[:, _col(2)] * _dot(h_s[...], wout_ref[...])
    y = _ffn_body(x1, mod_ref[:, _col(3)], mod_ref[:, _col(4)], mod_ref[:, _col(5)],
                  n2g_ref[this, :], w1_ref, w2_ref, h_s)
    if final:
        y = _rms(y, fg_ref[...])
    y_ref[...] = y


def _spost_call(layer, o, g, gb, ma, x, mod, gng, n2g, fg, w_out, w1, w2, *, final):
    n, d = x.shape
    depth = mod.shape[0]
    return pl.pallas_call(
        functools.partial(_spost_kernel, layer=layer, final=final),
        out_shape=jax.ShapeDtypeStruct((n, d), F32),
        grid=(1,),
        in_specs=[
            _full((n, d)), _full((n, d)), _full((n, d)), _full((n, d)), _full((n, d)),
            pl.BlockSpec((None, n, 6 * d), lambda i: (layer, 0, 0)),
            _full((depth, d)), _full((depth, d)),
            _full((1, d)),
            _matrix(d, d),
            _matrix(d, D_FF),
            _matrix(D_FF, d),
        ],
        out_specs=_full((n, d)),
        scratch_shapes=[pltpu.VMEM((n, d), BF16)],
        compiler_params=pltpu.CompilerParams(
            dimension_semantics=("arbitrary",), vmem_limit_bytes=VMEM_LIMIT_BYTES),
        name="spost",
    )(o, g, gb, ma, x, mod, gng, n2g, fg, w_out, w1, w2)


def _rope_tables(pos0, t):
    half = HEAD_DIM // 2
    inv = ROPE_BASE ** (-jnp.arange(half, dtype=F32) / half)
    pos = pos0 + jnp.arange(t, dtype=F32)
    ang = pos[:, None] * inv[None, :]
    return jnp.cos(ang), jnp.sin(ang)


def _retention_tables(length):
    log_g = jnp.log1p(-jnp.exp2(-5.0 - jnp.arange(HEADS, dtype=F32)))
    idx = jnp.arange(length, dtype=F32)
    diff = idx[:, None] - idx[None, :]
    decay = jnp.where(diff[None] >= 0.0,
                      jnp.exp(jnp.maximum(diff, 0.0)[None] * log_g[:, None, None]), 0.0)
    xi = jnp.exp((idx[:, None] + 1.0) * log_g[None, :])
    zeta = jnp.exp((length - 1.0 - idx)[:, None] * log_g[None, :])
    g_len = jnp.exp(length * log_g)
    return decay, xi, zeta, g_len


def kernel(x_prompt, x_sample, state_ret, c_prompt, c_sample, w_ada, b_ada, norm1_g, w_in, ln_v_g,
           ln_v_b, w_s, b_s, gn_g, w_out, norm2_g, w_ff1, w_ff2, final_g):
    depth = w_in.shape[0]
    batch, seq, d = x_prompt.shape
    n_sample = x_sample.shape[0]
    n_steps = batch * (seq // TILE_M)
    assert d == D_MODEL and seq % TILE_M == 0 and x_sample.shape[1] == 1
    assert TILE_M % RET_CHUNK == 0 and RET_CHUNK % CHUNK == 0
    assert n_sample % n_steps == 0

    w_in_b = w_in[0].astype(BF16)
    w_out_b = w_out[0].astype(BF16)
    n1g, lng, lnb, gng, n2g = norm1_g, ln_v_g, ln_v_b, gn_g, norm2_g
    fg = final_g.reshape(1, d)

    mod_p, mod_s = _ada_call(c_prompt, c_sample, w_ada, b_ada)

    cos_p, sin_p = _rope_tables(0.0, seq)
    cos_s, sin_s = _rope_tables(float(PAST_LEN), 1)
    decay, xi, zeta, g_len = _retention_tables(RET_CHUNK)
    xi_cols = jnp.repeat(xi, HEAD_DIM, axis=1)
    zeta_cols = jnp.repeat(zeta, HEAD_DIM, axis=1)
    decay1, xi1, zeta1, g_len1 = _retention_tables(1)
    consts1 = jnp.stack([decay1[:, 0, 0], xi1[0], zeta1[0], g_len1])
    b_s_t = jnp.swapaxes(b_s, 1, 2)
    w0_cols = jnp.repeat(w_s[:, :, 0, 0], GROUP_DIM, axis=1)
    b0_cols = jnp.repeat(b_s[:, :, 0], GROUP_DIM, axis=1)

    xp = x_prompt
    xs = x_sample.reshape(n_sample, d)
    ret_prompt = new_state = van_all = None
    per_step = n_sample // n_steps
    split = lambda a: a.reshape(per_step, n_steps, d)
    state_in = state_ret.reshape(depth, per_step, n_steps, *state_ret.shape[2:])
    for l in range(depth):
        final = l == depth - 1
        xp, ret_prompt, w_ff1_b, w_ff2_b = _mix_call(
            l, xp, mod_p, cos_p, sin_p, n1g, lng, lnb, gng, w_s, b_s_t, decay, xi_cols, zeta_cols,
            g_len, w_in_b, w_out_b, w_ff1, w_ff2, ret_prompt)
        van_all, ma, q, k, v, g, gb = _sproj_call(l, xs, mod_s, cos_s, sin_s, n1g, lng, lnb, w0_cols,
                                                  b0_cols, w_in_b, van_all)
        xp, o, new_state, *next_w = _ffn_call(
            l, xp, mod_p, n2g, fg, w_ff1_b, w_ff2_b, consts1, split(q), split(k), split(v),
            state_in, new_state, w_in, w_out, final=final)
        xs = _spost_call(l, o.reshape(n_sample, d), g, gb, ma, xs, mod_s, gng, n2g, fg, w_out_b,
                         w_ff1_b, w_ff2_b, final=final)
        if not final:
            w_in_b, w_out_b = next_w

    return (xp, xs.reshape(x_sample.shape), ret_prompt, new_state.reshape(state_ret.shape),
            van_all.reshape(depth, n_sample, 1, d))
```

```python
import functools

import jax
import jax.numpy as jnp
from jax import lax
from jax.experimental import pallas as pl
from jax.experimental.pallas import tpu as pltpu

F32 = jnp.float32
BF16 = jnp.bfloat16

D_MODEL = 1024
CHUNK = 128
GROUPS = 4
GROUP_DIM = D_MODEL // GROUPS
HEADS = 4
HEAD_DIM = D_MODEL // HEADS
D_FF = 4 * D_MODEL
N_SLABS = 8
SLAB_U, SLAB_VA, SLAB_Q, SLAB_K, SLAB_VR, SLAB_G, SLAB_GA, SLAB_GB = range(N_SLABS)
PAST_LEN = 16384
ROPE_BASE = 10000.0
EPS = 1e-6
SUBLANES = 8

RET_CHUNK = 256
TILE_M = 512
ADA_BLOCK_N = 1536
VMEM_LIMIT_BYTES = 56 * 1024 * 1024


def _resident(block_shape, index_map):
    return pl.BlockSpec(block_shape, index_map, pipeline_mode=pl.Buffered(1))


def _layer_matrix(layer, rows, cols):
    return _resident((None, rows, cols), lambda *_: (layer, 0, 0))


def _matrix(rows, cols):
    return _resident((rows, cols), lambda *_: (0, 0))


def _cast_specs(layer, rows, cols, n_steps, step):
    assert rows % n_steps == 0
    rb = rows // n_steps
    return (pl.BlockSpec((None, rb, cols), lambda *ids: (layer, step(*ids), 0)),
            pl.BlockSpec((rb, cols), lambda *ids: (step(*ids), 0)))


def _full(shape):
    return pl.BlockSpec(shape, lambda *_: (0,) * len(shape))


def _drop_first_ref(body):
    def wrapped(_, *refs):
        return body(*refs)
    return wrapped


def _call_carrying(body, carried, carried_out, *, in_specs, args, **kwargs):
    if carried is None:
        return pl.pallas_call(body, in_specs=in_specs, **kwargs)(*args)
    return pl.pallas_call(
        _drop_first_ref(body), in_specs=[pl.BlockSpec(memory_space=pl.ANY)] + in_specs,
        input_output_aliases={0: carried_out}, **kwargs)(carried, *args)


def _rms(x, g):
    return x * lax.rsqrt(jnp.mean(x * x, axis=-1, keepdims=True) + EPS) * g


def _modulated_rms(x, g, scale, shift):
    return x * lax.rsqrt(jnp.mean(x * x, axis=-1, keepdims=True) + EPS) * (g * (1.0 + scale)) + shift


def _center_scale(x):
    mu = jnp.mean(x, axis=-1, keepdims=True)
    xc = x - mu
    var = jnp.mean(xc * xc, axis=-1, keepdims=True)
    return xc * lax.rsqrt(var + EPS)


def _dot(a, b):
    return jnp.dot(a, b, preferred_element_type=F32)


def _col(i):
    return slice(i * D_MODEL, (i + 1) * D_MODEL)


def _head(hd):
    return slice(hd * HEAD_DIM, (hd + 1) * HEAD_DIM)


def _ada_kernel(cp_ref, cs_ref, w_ref, b_ref, op_ref, os_ref):
    n_prompt = cp_ref.shape[0]
    c = jnp.concatenate([cp_ref[...], cs_ref[...]], axis=0).astype(BF16)
    mod = _dot(c, w_ref[...].astype(BF16)) + b_ref[...]
    op_ref[...] = mod[:n_prompt]
    os_ref[...] = mod[n_prompt:]


def _ada_call(c_prompt, c_sample, w_ada, b_ada):
    depth, d, n = w_ada.shape
    bp, bs = c_prompt.shape[0], c_sample.shape[0]
    return pl.pallas_call(
        _ada_kernel,
        out_shape=(jax.ShapeDtypeStruct((depth, bp, n), F32),
                   jax.ShapeDtypeStruct((depth, bs, n), F32)),
        grid=(depth, n // ADA_BLOCK_N),
        in_specs=[
            _full((bp, d)),
            _full((bs, d)),
            pl.BlockSpec((None, d, ADA_BLOCK_N), lambda l, j: (l, 0, j)),
            pl.BlockSpec((None, 1, ADA_BLOCK_N), lambda l, j: (l, 0, j)),
        ],
        out_specs=(pl.BlockSpec((None, bp, ADA_BLOCK_N), lambda l, j: (l, 0, j)),
                   pl.BlockSpec((None, bs, ADA_BLOCK_N), lambda l, j: (l, 0, j))),
        compiler_params=pltpu.CompilerParams(
            dimension_semantics=("arbitrary", "arbitrary"), vmem_limit_bytes=VMEM_LIMIT_BYTES),
        name="ada",
    )(c_prompt, c_sample, w_ada, b_ada.reshape(depth, 1, n))


def _retention_one_token(c_ref, q_ref, k_ref, v_ref, s_ref, o_ref, sn_ref, row):
    n_rows = q_ref.shape[0]

    def rows_of(ref, hd):
        return jnp.concatenate([ref[a, pl.ds(row, 1), _head(hd)] for a in range(n_rows)], axis=0)

    for hd in range(HEADS):
        q = rows_of(q_ref, hd)
        k = rows_of(k_ref, hd)
        v = rows_of(v_ref, hd)
        decay, xi, zeta, g_l = c_ref[0, hd], c_ref[1, hd], c_ref[2, hd], c_ref[3, hd]
        sc = jnp.sum(q * k, axis=-1, keepdims=True) * decay
        pad = jnp.zeros((CHUNK - n_rows, HEAD_DIM), F32)
        q_t = jnp.concatenate([q, pad], axis=0).T
        kz_t = jnp.concatenate([k * zeta, pad], axis=0).T
        qs_rows = []
        for b in range(n_rows):
            state = s_ref[b, hd]
            qs_rows.append(jnp.sum(q_t[:, b:b + 1] * state, axis=0, keepdims=True))
            sn_ref[b, hd] = state * g_l + kz_t[:, b:b + 1] * v[b:b + 1, :]
        o = sc * v + jnp.concatenate(qs_rows, axis=0) * xi
        for a in range(n_rows):
            o_ref[a, pl.ds(row, 1), _head(hd)] = o[a:a + 1, :]


def _ffn_body(x, sh2, sc2, gt2, n2g, w1_ref, w2_ref, h_s):
    h_s[...] = _modulated_rms(x, n2g, sc2, sh2).astype(BF16)
    acc = None
    for j in range(D_FF // D_MODEL):
        hid = _dot(h_s[...], w1_ref[:, _col(j)])
        hid = jnp.square(jnp.maximum(hid, 0.0)).astype(BF16)
        part = _dot(hid, w2_ref[_col(j), :])
        acc = part if acc is None else acc + part
    return x + gt2 * acc


def _ffn_kernel(*refs, layer, final):
    (c_ref, x_ref, mod_ref, n2g_ref, fg_ref, w1_ref, w2_ref, q_ref, k_ref, v_ref, s_ref) = refs[:11]
    h_s = refs[-1]
    if final:
        y_ref, o_ref, sn_ref = refs[11:14]
    else:
        win_ref, wout_ref, y_ref, o_ref, sn_ref, winb_ref, woutb_ref = refs[11:18]
        winb_ref[...] = win_ref[...].astype(BF16)
        woutb_ref[...] = wout_ref[...].astype(BF16)
    step = pl.program_id(0) * pl.num_programs(1) + pl.program_id(1)
    _retention_one_token(c_ref, q_ref, k_ref, v_ref, s_ref, o_ref, sn_ref, step % SUBLANES)
    mod = mod_ref.at[pl.ds(pl.program_id(0), 1)]
    y = _ffn_body(x_ref[...], mod[:, _col(3)], mod[:, _col(4)], mod[:, _col(5)],
                  n2g_ref[layer:layer + 1, :], w1_ref, w2_ref, h_s)
    if final:
        y = _rms(y, fg_ref[...])
    y_ref[...] = y


def _ffn_call(layer, x, mod, n2g, fg, w1, w2, consts1, q, k, v, state_all, new_state_all, w_in, w_out,
              *, final):
    b, t, d = x.shape
    depth = mod.shape[0]
    tm = TILE_M
    nj = t // tm
    nb, n_steps, _ = q.shape
    assert n_steps == b * nj and n_steps % SUBLANES == 0
    step = lambda i, j: i * nj + j
    rows_spec = pl.BlockSpec((nb, SUBLANES, d), lambda i, j: (0, step(i, j) // SUBLANES, 0))
    state_spec = pl.BlockSpec((None, nb, None, HEADS, HEAD_DIM, HEAD_DIM),
                              lambda i, j: (layer, 0, step(i, j), 0, 0, 0))
    in_specs = [
        pl.BlockSpec(memory_space=pltpu.SMEM),
        pl.BlockSpec((None, tm, d), lambda i, j: (i, j, 0)),
        _layer_matrix(layer, b, 6 * d),
        _resident((depth, d), lambda i, j: (0, 0)),
        _full((1, d)),
        _matrix(d, D_FF),
        _matrix(D_FF, d),
        rows_spec, rows_spec, rows_spec,
        state_spec,
    ]
    args = [consts1, x, mod, n2g, fg, w1, w2, q, k, v, state_all]
    out_shape = [jax.ShapeDtypeStruct(x.shape, F32),
                 jax.ShapeDtypeStruct(q.shape, F32),
                 jax.ShapeDtypeStruct(state_all.shape, F32)]
    out_specs = [pl.BlockSpec((None, tm, d), lambda i, j: (i, j, 0)), rows_spec, state_spec]
    if not final:
        for w in (w_in, w_out):
            src, dst = _cast_specs(layer + 1, w.shape[1], w.shape[2], n_steps, step)
            in_specs.append(src)
            args.append(w)
            out_specs.append(dst)
            out_shape.append(jax.ShapeDtypeStruct(w.shape[1:], BF16))
    return _call_carrying(
        functools.partial(_ffn_kernel, layer=layer, final=final), new_state_all, 2,
        out_shape=tuple(out_shape),
        grid=(b, nj),
        in_specs=in_specs,
        args=args,
        out_specs=tuple(out_specs),
        scratch_shapes=[pltpu.VMEM((tm, d), BF16)],
        compiler_params=pltpu.CompilerParams(
            dimension_semantics=("arbitrary", "arbitrary"), vmem_limit_bytes=VMEM_LIMIT_BYTES),
        name="ffn",
    )


def _mix_kernel(gl_ref, x_ref, mod_ref, cos_ref, sin_ref, n1g_ref, lng_ref, lnb_ref, gng_ref,
                ws_ref, bst_ref, decay_ref, xi_ref, zeta_ref, win_ref, wout_ref, wf1_ref, wf2_ref,
                xo_ref, s_ref, wf1b_ref, wf2b_ref,
                h_s, p_s, van_s, q_s, k_s, kz_s, v_s, a_s, on_s, *, layer):
    n_chunks = x_ref.shape[0] // CHUNK
    this = slice(layer, layer + 1)

    @pl.when(pl.program_id(1) == 0)
    def _():
        s_ref[...] = jnp.zeros_like(s_ref)

    wf1b_ref[...] = wf1_ref[...].astype(BF16)
    wf2b_ref[...] = wf2_ref[...].astype(BF16)

    mod = mod_ref.at[pl.ds(pl.program_id(0), 1)]
    sh1 = mod[:, _col(0)]
    sc1 = mod[:, _col(1)]
    gt1 = mod[:, _col(2)]
    h_s[...] = _modulated_rms(x_ref[...], n1g_ref[this, :], sc1, sh1).astype(BF16)

    def proj(slab):
        return _dot(h_s[...], win_ref[:, _col(slab)])

    def rows(c):
        return slice(c * CHUNK, (c + 1) * CHUNK)

    van_s[...] = (_center_scale(proj(SLAB_VA)) * lng_ref[this, :] + lnb_ref[this, :]).astype(BF16)
    p_s[...] = proj(SLAB_U)
    tril = (lax.broadcasted_iota(jnp.int32, (CHUNK, CHUNK), 0)
            >= lax.broadcasted_iota(jnp.int32, (CHUNK, CHUNK), 1))
    for g in range(GROUPS):
        w_g = jnp.where(tril, ws_ref[g], 0.0).astype(BF16)
        gc = slice(g * GROUP_DIM, (g + 1) * GROUP_DIM)
        for c in range(n_chunks):
            z = _dot(w_g, van_s[rows(c), gc]) + bst_ref[:, g:g + 1]
            a_s[rows(c), gc] = p_s[rows(c), gc] * z
    a_s[...] = jax.nn.sigmoid(proj(SLAB_GA)) * a_s[...]

    half = HEAD_DIM // 2

    def rotate(c, hd, scale=None):
        lo = slice(hd * HEAD_DIM, hd * HEAD_DIM + half)
        hi = slice(hd * HEAD_DIM + half, (hd + 1) * HEAD_DIM)
        x1 = p_s[rows(c), lo]
        x2 = p_s[rows(c), hi]
        cos = cos_ref[rows(c), :]
        sin = sin_ref[rows(c), :]
        if scale is not None:
            cos = cos * scale
            sin = sin * scale
        return lo, hi, x1 * cos - x2 * sin, x1 * sin + x2 * cos

    p_s[...] = proj(SLAB_Q)
    for c in range(n_chunks):
        for hd in range(HEADS):
            lo, hi, r1, r2 = rotate(c, hd)
            q_s[rows(c), lo] = r1.astype(BF16)
            q_s[rows(c), hi] = r2.astype(BF16)
    p_s[...] = proj(SLAB_K)
    k_scale = HEAD_DIM ** -0.5
    for c in range(n_chunks):
        for hd in range(HEADS):
            lo, hi, r1, r2 = rotate(c, hd, k_scale)
            k_s[rows(c), lo] = r1.astype(BF16)
            k_s[rows(c), hi] = r2.astype(BF16)
            zr = slice(c * CHUNK % RET_CHUNK, c * CHUNK % RET_CHUNK + CHUNK)
            kz_s[rows(c), lo] = (r1 * zeta_ref[zr, lo]).astype(BF16)
            kz_s[rows(c), hi] = (r2 * zeta_ref[zr, hi]).astype(BF16)
    v_s[...] = proj(SLAB_VR).astype(BF16)

    for c in range(x_ref.shape[0] // RET_CHUNK):
        rc = slice(c * RET_CHUNK, (c + 1) * RET_CHUNK)
        for hd in range(HEADS):
            hc = _head(hd)
            qc = q_s[rc, hc]
            vc = v_s[rc, hc]
            state = s_ref[hd]
            sc = lax.dot_general(qc, k_s[rc, hc], (((1,), (1,)), ((), ())),
                                 preferred_element_type=F32) * decay_ref[hd]
            o = _dot(sc.astype(BF16), vc) + _dot(qc, state.astype(BF16)) * xi_ref[:, hc]
            s_ref[hd] = state * gl_ref[hd] + lax.dot_general(
                kz_s[rc, hc], vc, (((0,), (0,)), ((), ())), preferred_element_type=F32)
            on_s[rc, hc] = _center_scale(o) * gng_ref[this, hc]

    on_s[...] = jax.nn.silu(proj(SLAB_G)) * on_s[...]
    h_s[...] = (a_s[...] + jax.nn.sigmoid(proj(SLAB_GB)) * on_s[...]).astype(BF16)
    xo_ref[...] = x_ref[...] + gt1 * _dot(h_s[...], wout_ref[...])


def _mix_call(layer, x, mod, cos, sin, n1g, lng, lnb, gng, ws, bst, decay, xi, zeta, gl, w_in, w_out,
              w_ff1, w_ff2, ret_all):
    b, t, d = x.shape
    depth = w_ff1.shape[0]
    tm = TILE_M
    nj = t // tm
    step = lambda i, j: i * nj + j
    ff1_src, ff1_dst = _cast_specs(layer, d, D_FF, b * nj, step)
    ff2_src, ff2_dst = _cast_specs(layer, D_FF, d, b * nj, step)
    params = _resident((depth, d), lambda i, j: (0, 0))
    return _call_carrying(
        functools.partial(_mix_kernel, layer=layer), ret_all, 1,
        out_shape=(jax.ShapeDtypeStruct(x.shape, F32),
                   jax.ShapeDtypeStruct((depth, b, HEADS, HEAD_DIM, HEAD_DIM), F32),
                   jax.ShapeDtypeStruct((d, D_FF), BF16),
                   jax.ShapeDtypeStruct((D_FF, d), BF16)),
        grid=(b, nj),
        in_specs=[
            pl.BlockSpec(memory_space=pltpu.SMEM),
            pl.BlockSpec((None, tm, d), lambda i, j: (i, j, 0)),
            _layer_matrix(layer, b, 6 * d),
            pl.BlockSpec((tm, HEAD_DIM // 2), lambda i, j: (j, 0)),
            pl.BlockSpec((tm, HEAD_DIM // 2), lambda i, j: (j, 0)),
            params, params, params, params,
            _resident((None, GROUPS, CHUNK, CHUNK), lambda i, j: (layer, 0, 0, 0)),
            _layer_matrix(layer, CHUNK, GROUPS),
            _resident((HEADS, RET_CHUNK, RET_CHUNK), lambda i, j: (0, 0, 0)),
            _resident((RET_CHUNK, d), lambda i, j: (0, 0)),
            _resident((RET_CHUNK, d), lambda i, j: (0, 0)),
            _matrix(d, N_SLABS * d),
            _matrix(d, d),
            ff1_src, ff2_src,
        ],
        args=[gl, x, mod, cos, sin, n1g, lng, lnb, gng, ws, bst, decay, xi, zeta, w_in, w_out,
              w_ff1, w_ff2],
        out_specs=(
            pl.BlockSpec((None, tm, d), lambda i, j: (i, j, 0)),
            pl.BlockSpec((None, None, HEADS, HEAD_DIM, HEAD_DIM), lambda i, j: (layer, i, 0, 0, 0)),
            ff1_dst, ff2_dst,
        ),
        scratch_shapes=[
            pltpu.VMEM((tm, d), BF16),
            pltpu.VMEM((tm, d), F32),
            pltpu.VMEM((tm, d), BF16),
            pltpu.VMEM((tm, d), BF16),
            pltpu.VMEM((tm, d), BF16),
            pltpu.VMEM((tm, d), BF16),
            pltpu.VMEM((tm, d), BF16),
            pltpu.VMEM((tm, d), F32),
            pltpu.VMEM((tm, d), F32),
        ],
        compiler_params=pltpu.CompilerParams(
            dimension_semantics=("arbitrary", "arbitrary"), vmem_limit_bytes=VMEM_LIMIT_BYTES),
        name="mix",
    )


def _sproj_kernel(x_ref, mod_ref, cos_ref, sin_ref, n1g_ref, lng_ref, lnb_ref, w0_ref, b0_ref,
                  win_ref, van_ref, ma_ref, q_ref, k_ref, v_ref, g_ref, gb_ref, h_s, *, layer):
    this = slice(layer, layer + 1)
    sh1 = mod_ref[:, _col(0)]
    sc1 = mod_ref[:, _col(1)]
    h_s[...] = _modulated_rms(x_ref[...], n1g_ref[this, :], sc1, sh1).astype(BF16)

    def proj(slab):
        return _dot(h_s[...], win_ref[:, _col(slab)])

    van = _center_scale(proj(SLAB_VA)) * lng_ref[this, :] + lnb_ref[this, :]
    van_ref[...] = van
    z = van * w0_ref[this, :] + b0_ref[this, :]
    ma_ref[...] = jax.nn.sigmoid(proj(SLAB_GA)) * (proj(SLAB_U) * z)

    half = HEAD_DIM // 2
    cos = cos_ref[...]
    sin = sin_ref[...]

    def rotary_to(dst_ref, p, scale):
        for hd in range(HEADS):
            lo = slice(hd * HEAD_DIM, hd * HEAD_DIM + half)
            hi = slice(hd * HEAD_DIM + half, (hd + 1) * HEAD_DIM)
            x1 = p[:, lo]
            x2 = p[:, hi]
            dst_ref[:, lo] = (x1 * cos - x2 * sin) * scale
            dst_ref[:, hi] = (x1 * sin + x2 * cos) * scale

    rotary_to(q_ref, proj(SLAB_Q), 1.0)
    rotary_to(k_ref, proj(SLAB_K), HEAD_DIM ** -0.5)
    v_ref[...] = proj(SLAB_VR)
    g_ref[...] = proj(SLAB_G)
    gb_ref[...] = proj(SLAB_GB)


def _sproj_call(layer, x, mod, cos, sin, n1g, lng, lnb, w0, b0, w_in, van_all):
    n, d = x.shape
    depth = mod.shape[0]
    out = jax.ShapeDtypeStruct((n, d), F32)
    params = _full((depth, d))
    return _call_carrying(
        functools.partial(_sproj_kernel, layer=layer), van_all, 0,
        out_shape=(jax.ShapeDtypeStruct((depth, n, d), F32),) + (out,) * 6,
        grid=(1,),
        in_specs=[
            _full((n, d)),
            pl.BlockSpec((None, n, 6 * d), lambda i: (layer, 0, 0)),
            _full((1, HEAD_DIM // 2)), _full((1, HEAD_DIM // 2)),
            params, params, params, params, params,
            _matrix(d, N_SLABS * d),
        ],
        args=[x, mod, cos, sin, n1g, lng, lnb, w0, b0, w_in],
        out_specs=(pl.BlockSpec((None, n, d), lambda i: (layer, 0, 0)),) + (_full((n, d)),) * 6,
        scratch_shapes=[pltpu.VMEM((n, d), BF16)],
        compiler_params=pltpu.CompilerParams(
            dimension_semantics=("arbitrary",), vmem_limit_bytes=VMEM_LIMIT_BYTES),
        name="sproj",
    )


def _spost_kernel(o_ref, g_ref, gb_ref, ma_ref, x_ref, mod_ref, gng_ref, n2g_ref, fg_ref,
                  wout_ref, w1_ref, w2_ref, y_ref, h_s, *, layer, final):
    this = slice(layer, layer + 1)
    for hd in range(HEADS):
        hc = _head(hd)
        r = jax.nn.silu(g_ref[:, hc]) * (_center_scale(o_ref[:, hc]) * gng_ref[this, hc])
        h_s[:, hc] = (ma_ref[:, hc] + jax.nn.sigmoid(gb_ref[:, hc]) * r).astype(BF16)
    x1 = x_ref[...] + mod_ref[:, _col(2)] * _dot(h_s[...], wout_ref[...])
    y = _ffn_body(x1, mod_ref[:, _col(3)], mod_ref[:, _col(4)], mod_ref[:, _col(5)],
                  n2g_ref[this, :], w1_ref, w2_ref, h_s)
    if final:
        y = _rms(y, fg_ref[...])
    y_ref[...] = y


def _spost_call(layer, o, g, gb, ma, x, mod, gng, n2g, fg, w_out, w1, w2, *, final):
    n, d = x.shape
    depth = mod.shape[0]
    return pl.pallas_call(
        functools.partial(_spost_kernel, layer=layer, final=final),
        out_shape=jax.ShapeDtypeStruct((n, d), F32),
        grid=(1,),
        in_specs=[
            _full((n, d)), _full((n, d)), _full((n, d)), _full((n, d)), _full((n, d)),
            pl.BlockSpec((None, n, 6 * d), lambda i: (layer, 0, 0)),
            _full((depth, d)), _full((depth, d)),
            _full((1, d)),
            _matrix(d, d),
            _matrix(d, D_FF),
            _matrix(D_FF, d),
        ],
        out_specs=_full((n, d)),
        scratch_shapes=[pltpu.VMEM((n, d), BF16)],
        compiler_params=pltpu.CompilerParams(
            dimension_semantics=("arbitrary",), vmem_limit_bytes=VMEM_LIMIT_BYTES),
        name="spost",
    )(o, g, gb, ma, x, mod, gng, n2g, fg, w_out, w1, w2)


def _rope_tables(pos0, t):
    half = HEAD_DIM // 2
    inv = ROPE_BASE ** (-jnp.arange(half, dtype=F32) / half)
    pos = pos0 + jnp.arange(t, dtype=F32)
    ang = pos[:, None] * inv[None, :]
    return jnp.cos(ang), jnp.sin(ang)


def _retention_tables(length):
    log_g = jnp.log1p(-jnp.exp2(-5.0 - jnp.arange(HEADS, dtype=F32)))
    idx = jnp.arange(length, dtype=F32)
    diff = idx[:, None] - idx[None, :]
    decay = jnp.where(diff[None] >= 0.0,
                      jnp.exp(jnp.maximum(diff, 0.0)[None] * log_g[:, None, None]), 0.0)
    xi = jnp.exp((idx[:, None] + 1.0) * log_g[None, :])
    zeta = jnp.exp((length - 1.0 - idx)[:, None] * log_g[None, :])
    g_len = jnp.exp(length * log_g)
    return decay, xi, zeta, g_len


def kernel(x_prompt, x_sample, state_ret, c_prompt, c_sample, w_ada, b_ada, norm1_g, w_in, ln_v_g,
           ln_v_b, w_s, b_s, gn_g, w_out, norm2_g, w_ff1, w_ff2, final_g):
    depth = w_in.shape[0]
    batch, seq, d = x_prompt.shape
    n_sample = x_sample.shape[0]
    n_steps = batch * (seq // TILE_M)
    assert d == D_MODEL and seq % TILE_M == 0 and x_sample.shape[1] == 1
    assert TILE_M % RET_CHUNK == 0 and RET_CHUNK % CHUNK == 0
    assert n_sample % n_steps == 0

    w_in_b = w_in[0].astype(BF16)
    w_out_b = w_out[0].astype(BF16)
    n1g, lng, lnb, gng, n2g = norm1_g, ln_v_g, ln_v_b, gn_g, norm2_g
    fg = final_g.reshape(1, d)

    mod_p, mod_s = _ada_call(c_prompt, c_sample, w_ada, b_ada)

    cos_p, sin_p = _rope_tables(0.0, seq)
    cos_s, sin_s = _rope_tables(float(PAST_LEN), 1)
    decay, xi, zeta, g_len = _retention_tables(RET_CHUNK)
    xi_cols = jnp.repeat(xi, HEAD_DIM, axis=1)
    zeta_cols = jnp.repeat(zeta, HEAD_DIM, axis=1)
    decay1, xi1, zeta1, g_len1 = _retention_tables(1)
    consts1 = jnp.stack([decay1[:, 0, 0], xi1[0], zeta1[0], g_len1])
    b_s_t = jnp.swapaxes(b_s, 1, 2)
    w0_cols = jnp.repeat(w_s[:, :, 0, 0], GROUP_DIM, axis=1)
    b0_cols = jnp.repeat(b_s[:, :, 0], GROUP_DIM, axis=1)

    xp = x_prompt
    xs = x_sample.reshape(n_sample, d)
    ret_prompt = new_state = van_all = None
    per_step = n_sample // n_steps
    split = lambda a: a.reshape(per_step, n_steps, d)
    state_in = state_ret.reshape(depth, per_step, n_steps, *state_ret.shape[2:])
    for l in range(depth):
        final = l == depth - 1
        xp, ret_prompt, w_ff1_b, w_ff2_b = _mix_call(
            l, xp, mod_p, cos_p, sin_p, n1g, lng, lnb, gng, w_s, b_s_t, decay, xi_cols, zeta_cols,
            g_len, w_in_b, w_out_b, w_ff1, w_ff2, ret_prompt)
        van_all, ma, q, k, v, g, gb = _sproj_call(l, xs, mod_s, cos_s, sin_s, n1g, lng, lnb, w0_cols,
                                                  b0_cols, w_in_b, van_all)
        xp, o, new_state, *next_w = _ffn_call(
            l, xp, mod_p, n2g, fg, w_ff1_b, w_ff2_b, consts1, split(q), split(k), split(v),
            state_in, new_state, w_in, w_out, final=final)
        xs = _spost_call(l, o.reshape(n_sample, d), g, gb, ma, xs, mod_s, gng, n2g, fg, w_out_b,
                         w_ff1_b, w_ff2_b, final=final)
        if not final:
            w_in_b, w_out_b = next_w

    return (xp, xs.reshape(x_sample.shape), ret_prompt, new_state.reshape(state_ret.shape),
            van_all.reshape(depth, n_sample, 1, d))
```

```python
import functools

import jax
import jax.numpy as jnp
from jax import lax
from jax.experimental import pallas as pl
from jax.experimental.pallas import tpu as pltpu

F32 = jnp.float32
BF16 = jnp.bfloat16

D_MODEL = 1024
CHUNK = 128
GROUPS = 4
GROUP_DIM = D_MODEL // GROUPS
HEADS = 4
HEAD_DIM = D_MODEL // HEADS
D_FF = 4 * D_MODEL
N_SLABS = 8
SLAB_U, SLAB_VA, SLAB_Q, SLAB_K, SLAB_VR, SLAB_G, SLAB_GA, SLAB_GB = range(N_SLABS)
PAST_LEN = 16384
ROPE_BASE = 10000.0
EPS = 1e-6
SUBLANES = 8

RET_CHUNK = 256
TILE_M = 512
ADA_BLOCK_N = 3072
VMEM_LIMIT_BYTES = 56 * 1024 * 1024


def _resident(block_shape, index_map):
    return pl.BlockSpec(block_shape, index_map, pipeline_mode=pl.Buffered(1))


def _layer_matrix(layer, rows, cols):
    return _resident((None, rows, cols), lambda *_: (layer, 0, 0))


def _matrix(rows, cols):
    return _resident((rows, cols), lambda *_: (0, 0))


def _cast_specs(layer, rows, cols, n_steps, step):
    assert rows % n_steps == 0
    rb = rows // n_steps
    return (pl.BlockSpec((None, rb, cols), lambda *ids: (layer, step(*ids), 0)),
            pl.BlockSpec((rb, cols), lambda *ids: (step(*ids), 0)))


def _full(shape):
    return pl.BlockSpec(shape, lambda *_: (0,) * len(shape))


def _drop_first_ref(body):
    def wrapped(_, *refs):
        return body(*refs)
    return wrapped


def _call_carrying(body, carried, carried_out, *, in_specs, args, **kwargs):
    if carried is None:
        return pl.pallas_call(body, in_specs=in_specs, **kwargs)(*args)
    return pl.pallas_call(
        _drop_first_ref(body), in_specs=[pl.BlockSpec(memory_space=pl.ANY)] + in_specs,
        input_output_aliases={0: carried_out}, **kwargs)(carried, *args)


def _rms(x, g):
    return x * lax.rsqrt(jnp.mean(x * x, axis=-1, keepdims=True) + EPS) * g


def _modulated_rms(x, g, scale, shift):
    return x * lax.rsqrt(jnp.mean(x * x, axis=-1, keepdims=True) + EPS) * (g * (1.0 + scale)) + shift


def _center_scale(x):
    mu = jnp.mean(x, axis=-1, keepdims=True)
    xc = x - mu
    var = jnp.mean(xc * xc, axis=-1, keepdims=True)
    return xc * lax.rsqrt(var + EPS)


def _dot(a, b):
    return jnp.dot(a, b, preferred_element_type=F32)


def _col(i):
    return slice(i * D_MODEL, (i + 1) * D_MODEL)


def _head(hd):
    return slice(hd * HEAD_DIM, (hd + 1) * HEAD_DIM)


def _ada_kernel(cp_ref, cs_ref, w_ref, b_ref, op_ref, os_ref):
    n_prompt = cp_ref.shape[0]
    c = jnp.concatenate([cp_ref[...], cs_ref[...]], axis=0).astype(BF16)
    mod = _dot(c, w_ref[...].astype(BF16)) + b_ref[...]
    op_ref[...] = mod[:n_prompt]
    os_ref[...] = mod[n_prompt:]


def _ada_call(c_prompt, c_sample, w_ada, b_ada):
    depth, d, n = w_ada.shape
    bp, bs = c_prompt.shape[0], c_sample.shape[0]
    return pl.pallas_call(
        _ada_kernel,
        out_shape=(jax.ShapeDtypeStruct((depth, bp, n), F32),
                   jax.ShapeDtypeStruct((depth, bs, n), F32)),
        grid=(depth, n // ADA_BLOCK_N),
        in_specs=[
            _full((bp, d)),
            _full((bs, d)),
            pl.BlockSpec((None, d, ADA_BLOCK_N), lambda l, j: (l, 0, j)),
            pl.BlockSpec((None, 1, ADA_BLOCK_N), lambda l, j: (l, 0, j)),
        ],
        out_specs=(pl.BlockSpec((None, bp, ADA_BLOCK_N), lambda l, j: (l, 0, j)),
                   pl.BlockSpec((None, bs, ADA_BLOCK_N), lambda l, j: (l, 0, j))),
        compiler_params=pltpu.CompilerParams(
            dimension_semantics=("arbitrary", "arbitrary"), vmem_limit_bytes=VMEM_LIMIT_BYTES),
        name="ada",
    )(c_prompt, c_sample, w_ada, b_ada.reshape(depth, 1, n))


def _retention_one_token(c_ref, q_ref, k_ref, v_ref, s_ref, o_ref, sn_ref, row):
    n_rows = q_ref.shape[0]

    def rows_of(ref, hd):
        return jnp.concatenate([ref[a, pl.ds(row, 1), _head(hd)] for a in range(n_rows)], axis=0)

    for hd in range(HEADS):
        q = rows_of(q_ref, hd)
        k = rows_of(k_ref, hd)
        v = rows_of(v_ref, hd)
        decay, xi, zeta, g_l = c_ref[0, hd], c_ref[1, hd], c_ref[2, hd], c_ref[3, hd]
        sc = jnp.sum(q * k, axis=-1, keepdims=True) * decay
        pad = jnp.zeros((CHUNK - n_rows, HEAD_DIM), F32)
        q_t = jnp.concatenate([q, pad], axis=0).T
        kz_t = jnp.concatenate([k * zeta, pad], axis=0).T
        qs_rows = []
        for b in range(n_rows):
            state = s_ref[b, hd]
            qs_rows.append(jnp.sum(q_t[:, b:b + 1] * state, axis=0, keepdims=True))
            sn_ref[b, hd] = state * g_l + kz_t[:, b:b + 1] * v[b:b + 1, :]
        o = sc * v + jnp.concatenate(qs_rows, axis=0) * xi
        for a in range(n_rows):
            o_ref[a, pl.ds(row, 1), _head(hd)] = o[a:a + 1, :]


def _ffn_body(x, sh2, sc2, gt2, n2g, w1_ref, w2_ref, h_s, before_slab=None):
    h_s[...] = _modulated_rms(x, n2g, sc2, sh2).astype(BF16)
    acc = None
    for j in range(D_FF // D_MODEL):
        if before_slab is not None:
            before_slab(j)
        hid = _dot(h_s[...], w1_ref[:, _col(j)])
        hid = jnp.square(jnp.maximum(hid, 0.0)).astype(BF16)
        part = _dot(hid, w2_ref[_col(j), :])
        acc = part if acc is None else acc + part
    return x + gt2 * acc


def _ffn_kernel(*refs, layer, final):
    (c_ref, x_ref, mod_ref, n2g_ref, fg_ref, w1_ref, w2_ref, q_ref, k_ref, v_ref, s_ref) = refs[:11]
    h_s = refs[-1]
    if final:
        y_ref, o_ref, sn_ref = refs[11:14]
    else:
        win_ref, wout_ref, y_ref, o_ref, sn_ref, winb_ref, woutb_ref = refs[11:18]
        winb_ref[...] = win_ref[...].astype(BF16)
        woutb_ref[...] = wout_ref[...].astype(BF16)
    step = pl.program_id(0) * pl.num_programs(1) + pl.program_id(1)
    _retention_one_token(c_ref, q_ref, k_ref, v_ref, s_ref, o_ref, sn_ref, step % SUBLANES)
    mod = mod_ref.at[pl.ds(pl.program_id(0), 1)]
    y = _ffn_body(x_ref[...], mod[:, _col(3)], mod[:, _col(4)], mod[:, _col(5)],
                  n2g_ref[layer:layer + 1, :], w1_ref, w2_ref, h_s)
    if final:
        y = _rms(y, fg_ref[...])
    y_ref[...] = y


def _ffn_call(layer, x, mod, n2g, fg, w1, w2, consts1, q, k, v, state_all, new_state_all, w_in, w_out,
              *, final):
    b, t, d = x.shape
    depth = mod.shape[0]
    tm = TILE_M
    nj = t // tm
    nb, n_steps, _ = q.shape
    assert n_steps == b * nj and n_steps % SUBLANES == 0
    step = lambda i, j: i * nj + j
    rows_spec = pl.BlockSpec((nb, SUBLANES, d), lambda i, j: (0, step(i, j) // SUBLANES, 0))
    state_spec = pl.BlockSpec((None, nb, None, HEADS, HEAD_DIM, HEAD_DIM),
                              lambda i, j: (layer, 0, step(i, j), 0, 0, 0))
    in_specs = [
        pl.BlockSpec(memory_space=pltpu.SMEM),
        pl.BlockSpec((None, tm, d), lambda i, j: (i, j, 0)),
        _layer_matrix(layer, b, 6 * d),
        _resident((depth, d), lambda i, j: (0, 0)),
        _full((1, d)),
        _matrix(d, D_FF),
        _matrix(D_FF, d),
        rows_spec, rows_spec, rows_spec,
        state_spec,
    ]
    args = [consts1, x, mod, n2g, fg, w1, w2, q, k, v, state_all]
    out_shape = [jax.ShapeDtypeStruct(x.shape, F32),
                 jax.ShapeDtypeStruct(q.shape, F32),
                 jax.ShapeDtypeStruct(state_all.shape, F32)]
    out_specs = [pl.BlockSpec((None, tm, d), lambda i, j: (i, j, 0)), rows_spec, state_spec]
    if not final:
        for w in (w_in, w_out):
            src, dst = _cast_specs(layer + 1, w.shape[1], w.shape[2], n_steps, step)
            in_specs.append(src)
            args.append(w)
            out_specs.append(dst)
            out_shape.append(jax.ShapeDtypeStruct(w.shape[1:], BF16))
    return _call_carrying(
        functools.partial(_ffn_kernel, layer=layer, final=final), new_state_all, 2,
        out_shape=tuple(out_shape),
        grid=(b, nj),
        in_specs=in_specs,
        args=args,
        out_specs=tuple(out_specs),
        scratch_shapes=[pltpu.VMEM((tm, d), BF16)],
        compiler_params=pltpu.CompilerParams(
            dimension_semantics=("arbitrary", "arbitrary"), vmem_limit_bytes=VMEM_LIMIT_BYTES),
        name="ffn",
    )


def _mix_kernel(gl_ref, x_ref, mod_ref, cos_ref, sin_ref, n1g_ref, lng_ref, lnb_ref, gng_ref,
                ws_ref, bst_ref, decay_ref, xi_ref, zeta_ref, win_ref, wout_ref, wf1_ref, wf2_ref,
                xo_ref, s_ref, wf1b_ref, wf2b_ref,
                h_s, p_s, van_s, q_s, k_s, kz_s, v_s, a_s, on_s, *, layer):
    n_chunks = x_ref.shape[0] // CHUNK
    this = slice(layer, layer + 1)

    @pl.when(pl.program_id(1) == 0)
    def _():
        s_ref[...] = jnp.zeros_like(s_ref)

    wf1b_ref[...] = wf1_ref[...].astype(BF16)
    wf2b_ref[...] = wf2_ref[...].astype(BF16)

    mod = mod_ref.at[pl.ds(pl.program_id(0), 1)]
    sh1 = mod[:, _col(0)]
    sc1 = mod[:, _col(1)]
    gt1 = mod[:, _col(2)]
    h_s[...] = _modulated_rms(x_ref[...], n1g_ref[this, :], sc1, sh1).astype(BF16)

    def proj(slab):
        return _dot(h_s[...], win_ref[:, _col(slab)])

    def rows(c):
        return slice(c * CHUNK, (c + 1) * CHUNK)

    van_s[...] = (_center_scale(proj(SLAB_VA)) * lng_ref[this, :] + lnb_ref[this, :]).astype(BF16)
    p_s[...] = proj(SLAB_U)
    tril = (lax.broadcasted_iota(jnp.int32, (CHUNK, CHUNK), 0)
            >= lax.broadcasted_iota(jnp.int32, (CHUNK, CHUNK), 1))
    for g in range(GROUPS):
        w_g = jnp.where(tril, ws_ref[g], 0.0).astype(BF16)
        gc = slice(g * GROUP_DIM, (g + 1) * GROUP_DIM)
        for c in range(n_chunks):
            z = _dot(w_g, van_s[rows(c), gc]) + bst_ref[:, g:g + 1]
            a_s[rows(c), gc] = p_s[rows(c), gc] * z
    a_s[...] = jax.nn.sigmoid(proj(SLAB_GA)) * a_s[...]

    half = HEAD_DIM // 2

    def rotate(c, hd, scale=None):
        lo = slice(hd * HEAD_DIM, hd * HEAD_DIM + half)
        hi = slice(hd * HEAD_DIM + half, (hd + 1) * HEAD_DIM)
        x1 = p_s[rows(c), lo]
        x2 = p_s[rows(c), hi]
        cos = cos_ref[rows(c), :]
        sin = sin_ref[rows(c), :]
        if scale is not None:
            cos = cos * scale
            sin = sin * scale
        return lo, hi, x1 * cos - x2 * sin, x1 * sin + x2 * cos

    p_s[...] = proj(SLAB_Q)
    for c in range(n_chunks):
        for hd in range(HEADS):
            lo, hi, r1, r2 = rotate(c, hd)
            q_s[rows(c), lo] = r1.astype(BF16)
            q_s[rows(c), hi] = r2.astype(BF16)
    p_s[...] = proj(SLAB_K)
    k_scale = HEAD_DIM ** -0.5
    for c in range(n_chunks):
        for hd in range(HEADS):
            lo, hi, r1, r2 = rotate(c, hd, k_scale)
            k_s[rows(c), lo] = r1.astype(BF16)
            k_s[rows(c), hi] = r2.astype(BF16)
            zr = slice(c * CHUNK % RET_CHUNK, c * CHUNK % RET_CHUNK + CHUNK)
            kz_s[rows(c), lo] = (r1 * zeta_ref[zr, lo]).astype(BF16)
            kz_s[rows(c), hi] = (r2 * zeta_ref[zr, hi]).astype(BF16)
    v_s[...] = proj(SLAB_VR).astype(BF16)

    for c in range(x_ref.shape[0] // RET_CHUNK):
        rc = slice(c * RET_CHUNK, (c + 1) * RET_CHUNK)
        for hd in range(HEADS):
            hc = _head(hd)
            qc = q_s[rc, hc]
            vc = v_s[rc, hc]
            state = s_ref[hd]
            sc = lax.dot_general(qc, k_s[rc, hc], (((1,), (1,)), ((), ())),
                                 preferred_element_type=F32) * decay_ref[hd]
            o = _dot(sc.astype(BF16), vc) + _dot(qc, state.astype(BF16)) * xi_ref[:, hc]
            s_ref[hd] = state * gl_ref[hd] + lax.dot_general(
                kz_s[rc, hc], vc, (((0,), (0,)), ((), ())), preferred_element_type=F32)
            on_s[rc, hc] = _center_scale(o) * gng_ref[this, hc]

    on_s[...] = jax.nn.silu(proj(SLAB_G)) * on_s[...]
    h_s[...] = (a_s[...] + jax.nn.sigmoid(proj(SLAB_GB)) * on_s[...]).astype(BF16)
    xo_ref[...] = x_ref[...] + gt1 * _dot(h_s[...], wout_ref[...])


def _mix_call(layer, x, mod, cos, sin, n1g, lng, lnb, gng, ws, bst, decay, xi, zeta, gl, w_in, w_out,
              w_ff1, w_ff2, ret_all):
    b, t, d = x.shape
    depth = w_ff1.shape[0]
    tm = TILE_M
    nj = t // tm
    step = lambda i, j: i * nj + j
    ff1_src, ff1_dst = _cast_specs(layer, d, D_FF, b * nj, step)
    ff2_src, ff2_dst = _cast_specs(layer, D_FF, d, b * nj, step)
    params = _resident((depth, d), lambda i, j: (0, 0))
    return _call_carrying(
        functools.partial(_mix_kernel, layer=layer), ret_all, 1,
        out_shape=(jax.ShapeDtypeStruct(x.shape, F32),
                   jax.ShapeDtypeStruct((depth, b, HEADS, HEAD_DIM, HEAD_DIM), F32),
                   jax.ShapeDtypeStruct((d, D_FF), BF16),
                   jax.ShapeDtypeStruct((D_FF, d), BF16)),
        grid=(b, nj),
        in_specs=[
            pl.BlockSpec(memory_space=pltpu.SMEM),
            pl.BlockSpec((None, tm, d), lambda i, j: (i, j, 0)),
            _layer_matrix(layer, b, 6 * d),
            pl.BlockSpec((tm, HEAD_DIM // 2), lambda i, j: (j, 0)),
            pl.BlockSpec((tm, HEAD_DIM // 2), lambda i, j: (j, 0)),
            params, params, params, params,
            _resident((None, GROUPS, CHUNK, CHUNK), lambda i, j: (layer, 0, 0, 0)),
            _layer_matrix(layer, CHUNK, GROUPS),
            _resident((HEADS, RET_CHUNK, RET_CHUNK), lambda i, j: (0, 0, 0)),
            _resident((RET_CHUNK, d), lambda i, j: (0, 0)),
            _resident((RET_CHUNK, d), lambda i, j: (0, 0)),
            _matrix(d, N_SLABS * d),
            _matrix(d, d),
            ff1_src, ff2_src,
        ],
        args=[gl, x, mod, cos, sin, n1g, lng, lnb, gng, ws, bst, decay, xi, zeta, w_in, w_out,
              w_ff1, w_ff2],
        out_specs=(
            pl.BlockSpec((None, tm, d), lambda i, j: (i, j, 0)),
            pl.BlockSpec((None, None, HEADS, HEAD_DIM, HEAD_DIM), lambda i, j: (layer, i, 0, 0, 0)),
            ff1_dst, ff2_dst,
        ),
        scratch_shapes=[
            pltpu.VMEM((tm, d), BF16),
            pltpu.VMEM((tm, d), F32),
            pltpu.VMEM((tm, d), BF16),
            pltpu.VMEM((tm, d), BF16),
            pltpu.VMEM((tm, d), BF16),
            pltpu.VMEM((tm, d), BF16),
            pltpu.VMEM((tm, d), BF16),
            pltpu.VMEM((tm, d), F32),
            pltpu.VMEM((tm, d), F32),
        ],
        compiler_params=pltpu.CompilerParams(
            dimension_semantics=("arbitrary", "arbitrary"), vmem_limit_bytes=VMEM_LIMIT_BYTES),
        name="mix",
    )


def _sproj_kernel(x_ref, mod_ref, cos_ref, sin_ref, n1g_ref, lng_ref, lnb_ref, w0_ref, b0_ref,
                  win_hbm, van_ref, ma_ref, q_ref, k_ref, v_ref, g_ref, gb_ref, h_s, win_v, sems,
                  *, layer):
    this = slice(layer, layer + 1)
    order = (SLAB_VA, SLAB_GA, SLAB_U, SLAB_Q, SLAB_K, SLAB_VR, SLAB_G, SLAB_GB)
    copies = {s: pltpu.make_async_copy(win_hbm.at[:, _col(s)], win_v.at[:, _col(s)], sems.at[s])
              for s in order}
    for s in order:
        copies[s].start()

    sh1 = mod_ref[:, _col(0)]
    sc1 = mod_ref[:, _col(1)]
    h_s[...] = _modulated_rms(x_ref[...], n1g_ref[this, :], sc1, sh1).astype(BF16)

    def proj(slab):
        copies[slab].wait()
        return _dot(h_s[...], win_v[:, _col(slab)])

    van = _center_scale(proj(SLAB_VA)) * lng_ref[this, :] + lnb_ref[this, :]
    van_ref[...] = van
    z = van * w0_ref[this, :] + b0_ref[this, :]
    ma_ref[...] = jax.nn.sigmoid(proj(SLAB_GA)) * (proj(SLAB_U) * z)

    half = HEAD_DIM // 2
    cos = cos_ref[...]
    sin = sin_ref[...]

    def rotary_to(dst_ref, p, scale):
        for hd in range(HEADS):
            lo = slice(hd * HEAD_DIM, hd * HEAD_DIM + half)
            hi = slice(hd * HEAD_DIM + half, (hd + 1) * HEAD_DIM)
            x1 = p[:, lo]
            x2 = p[:, hi]
            dst_ref[:, lo] = (x1 * cos - x2 * sin) * scale
            dst_ref[:, hi] = (x1 * sin + x2 * cos) * scale

    rotary_to(q_ref, proj(SLAB_Q), 1.0)
    rotary_to(k_ref, proj(SLAB_K), HEAD_DIM ** -0.5)
    v_ref[...] = proj(SLAB_VR)
    g_ref[...] = proj(SLAB_G)
    gb_ref[...] = proj(SLAB_GB)


def _sproj_call(layer, x, mod, cos, sin, n1g, lng, lnb, w0, b0, w_in, van_all):
    n, d = x.shape
    depth = mod.shape[0]
    out = jax.ShapeDtypeStruct((n, d), F32)
    params = _full((depth, d))
    return _call_carrying(
        functools.partial(_sproj_kernel, layer=layer), van_all, 0,
        out_shape=(jax.ShapeDtypeStruct((depth, n, d), F32),) + (out,) * 6,
        grid=(1,),
        in_specs=[
            _full((n, d)),
            pl.BlockSpec((None, n, 6 * d), lambda i: (layer, 0, 0)),
            _full((1, HEAD_DIM // 2)), _full((1, HEAD_DIM // 2)),
            params, params, params, params, params,
            pl.BlockSpec(memory_space=pl.ANY),
        ],
        args=[x, mod, cos, sin, n1g, lng, lnb, w0, b0, w_in],
        out_specs=(pl.BlockSpec((None, n, d), lambda i: (layer, 0, 0)),) + (_full((n, d)),) * 6,
        scratch_shapes=[pltpu.VMEM((n, d), BF16),
                        pltpu.VMEM((d, N_SLABS * d), BF16),
                        pltpu.SemaphoreType.DMA((N_SLABS,))],
        compiler_params=pltpu.CompilerParams(
            dimension_semantics=("arbitrary",), vmem_limit_bytes=VMEM_LIMIT_BYTES),
        name="sproj",
    )


def _spost_kernel(o_ref, g_ref, gb_ref, ma_ref, x_ref, mod_ref, gng_ref, n2g_ref, fg_ref,
                  wout_hbm, w1_hbm, w2_hbm, y_ref, h_s, wout_v, w1_v, w2_v, sems, *, layer, final):
    this = slice(layer, layer + 1)
    n_slabs = D_FF // D_MODEL
    copy_out = pltpu.make_async_copy(wout_hbm, wout_v, sems.at[0])
    copy_w1 = [pltpu.make_async_copy(w1_hbm.at[:, _col(j)], w1_v.at[:, _col(j)], sems.at[1 + j])
               for j in range(n_slabs)]
    copy_w2 = [pltpu.make_async_copy(w2_hbm.at[_col(j), :], w2_v.at[_col(j), :],
                                     sems.at[1 + n_slabs + j]) for j in range(n_slabs)]
    copy_out.start()
    for j in range(n_slabs):
        copy_w1[j].start()
        copy_w2[j].start()

    def wait_slab(j):
        copy_w1[j].wait()
        copy_w2[j].wait()

    for hd in range(HEADS):
        hc = _head(hd)
        r = jax.nn.silu(g_ref[:, hc]) * (_center_scale(o_ref[:, hc]) * gng_ref[this, hc])
        h_s[:, hc] = (ma_ref[:, hc] + jax.nn.sigmoid(gb_ref[:, hc]) * r).astype(BF16)
    copy_out.wait()
    x1 = x_ref[...] + mod_ref[:, _col(2)] * _dot(h_s[...], wout_v[...])
    y = _ffn_body(x1, mod_ref[:, _col(3)], mod_ref[:, _col(4)], mod_ref[:, _col(5)],
                  n2g_ref[this, :], w1_v, w2_v, h_s, wait_slab)
    if final:
        y = _rms(y, fg_ref[...])
    y_ref[...] = y


def _spost_call(layer, o, g, gb, ma, x, mod, gng, n2g, fg, w_out, w1, w2, *, final):
    n, d = x.shape
    depth = mod.shape[0]
    return pl.pallas_call(
        functools.partial(_spost_kernel, layer=layer, final=final),
        out_shape=jax.ShapeDtypeStruct((n, d), F32),
        grid=(1,),
        in_specs=[
            _full((n, d)), _full((n, d)), _full((n, d)), _full((n, d)), _full((n, d)),
            pl.BlockSpec((None, n, 6 * d), lambda i: (layer, 0, 0)),
            _full((depth, d)), _full((depth, d)),
            _full((1, d)),
            pl.BlockSpec(memory_space=pl.ANY),
            pl.BlockSpec(memory_space=pl.ANY),
            pl.BlockSpec(memory_space=pl.ANY),
        ],
        out_specs=_full((n, d)),
        scratch_shapes=[pltpu.VMEM((n, d), BF16),
                        pltpu.VMEM((d, d), BF16),
                        pltpu.VMEM((d, D_FF), BF16),
                        pltpu.VMEM((D_FF, d), BF16),
                        pltpu.SemaphoreType.DMA((1 + 2 * (D_FF // D_MODEL),))],
        compiler_params=pltpu.CompilerParams(
            dimension_semantics=("arbitrary",), vmem_limit_bytes=VMEM_LIMIT_BYTES),
        name="spost",
    )(o, g, gb, ma, x, mod, gng, n2g, fg, w_out, w1, w2)


def _rope_tables(pos0, t):
    half = HEAD_DIM // 2
    inv = ROPE_BASE ** (-jnp.arange(half, dtype=F32) / half)
    pos = pos0 + jnp.arange(t, dtype=F32)
    ang = pos[:, None] * inv[None, :]
    return jnp.cos(ang), jnp.sin(ang)


def _retention_tables(length):
    log_g = jnp.log1p(-jnp.exp2(-5.0 - jnp.arange(HEADS, dtype=F32)))
    idx = jnp.arange(length, dtype=F32)
    diff = idx[:, None] - idx[None, :]
    decay = jnp.where(diff[None] >= 0.0,
                      jnp.exp(jnp.maximum(diff, 0.0)[None] * log_g[:, None, None]), 0.0)
    xi = jnp.exp((idx[:, None] + 1.0) * log_g[None, :])
    zeta = jnp.exp((length - 1.0 - idx)[:, None] * log_g[None, :])
    g_len = jnp.exp(length * log_g)
    return decay, xi, zeta, g_len


def kernel(x_prompt, x_sample, state_ret, c_prompt, c_sample, w_ada, b_ada, norm1_g, w_in, ln_v_g,
           ln_v_b, w_s, b_s, gn_g, w_out, norm2_g, w_ff1, w_ff2, final_g):
    depth = w_in.shape[0]
    batch, seq, d = x_prompt.shape
    n_sample = x_sample.shape[0]
    n_steps = batch * (seq // TILE_M)
    assert d == D_MODEL and seq % TILE_M == 0 and x_sample.shape[1] == 1
    assert TILE_M % RET_CHUNK == 0 and RET_CHUNK % CHUNK == 0
    assert n_sample % n_steps == 0

    w_in_b = w_in[0].astype(BF16)
    w_out_b = w_out[0].astype(BF16)
    n1g, lng, lnb, gng, n2g = norm1_g, ln_v_g, ln_v_b, gn_g, norm2_g
    fg = final_g.reshape(1, d)

    mod_p, mod_s = _ada_call(c_prompt, c_sample, w_ada, b_ada)

    cos_p, sin_p = _rope_tables(0.0, seq)
    cos_s, sin_s = _rope_tables(float(PAST_LEN), 1)
    decay, xi, zeta, g_len = _retention_tables(RET_CHUNK)
    xi_cols = jnp.repeat(xi, HEAD_DIM, axis=1)
    zeta_cols = jnp.repeat(zeta, HEAD_DIM, axis=1)
    decay1, xi1, zeta1, g_len1 = _retention_tables(1)
    consts1 = jnp.stack([decay1[:, 0, 0], xi1[0], zeta1[0], g_len1])
    b_s_t = jnp.swapaxes(b_s, 1, 2)
    w0_cols = jnp.repeat(w_s[:, :, 0, 0], GROUP_DIM, axis=1)
    b0_cols = jnp.repeat(b_s[:, :, 0], GROUP_DIM, axis=1)

    xp = x_prompt
    xs = x_sample.reshape(n_sample, d)
    ret_prompt = new_state = van_all = None
    per_step = n_sample // n_steps
    split = lambda a: a.reshape(per_step, n_steps, d)
    state_in = state_ret.reshape(depth, per_step, n_steps, *state_ret.shape[2:])
    for l in range(depth):
        final = l == depth - 1
        xp, ret_prompt, w_ff1_b, w_ff2_b = _mix_call(
            l, xp, mod_p, cos_p, sin_p, n1g, lng, lnb, gng, w_s, b_s_t, decay, xi_cols, zeta_cols,
            g_len, w_in_b, w_out_b, w_ff1, w_ff2, ret_prompt)
        van_all, ma, q, k, v, g, gb = _sproj_call(l, xs, mod_s, cos_s, sin_s, n1g, lng, lnb, w0_cols,
                                                  b0_cols, w_in_b, van_all)
        xp, o, new_state, *next_w = _ffn_call(
            l, xp, mod_p, n2g, fg, w_ff1_b, w_ff2_b, consts1, split(q), split(k), split(v),
            state_in, new_state, w_in, w_out, final=final)
        xs = _spost_call(l, o.reshape(n_sample, d), g, gb, ma, xs, mod_s, gng, n2g, fg, w_out_b,
                         w_ff1_b, w_ff2_b, final=final)
        if not final:
            w_in_b, w_out_b = next_w

    return (xp, xs.reshape(x_sample.shape), ret_prompt, new_state.reshape(state_ret.shape),
            van_all.reshape(depth, n_sample, 1, d))
```

```python
import functools

import jax
import jax.numpy as jnp
from jax import lax
from jax.experimental import pallas as pl
from jax.experimental.pallas import tpu as pltpu

F32 = jnp.float32
BF16 = jnp.bfloat16

D_MODEL = 1024
CHUNK = 128
GROUPS = 4
GROUP_DIM = D_MODEL // GROUPS
HEADS = 4
HEAD_DIM = D_MODEL // HEADS
D_FF = 4 * D_MODEL
N_SLABS = 8
SLAB_U, SLAB_VA, SLAB_Q, SLAB_K, SLAB_VR, SLAB_G, SLAB_GA, SLAB_GB = range(N_SLABS)
PAST_LEN = 16384
ROPE_BASE = 10000.0
EPS = 1e-6
SUBLANES = 8

RET_CHUNK = 256
TILE_M = 512
FFN_SLAB = 2048
ADA_BLOCK_N = 1536
VMEM_LIMIT_BYTES = 56 * 1024 * 1024


def _resident(block_shape, index_map):
    return pl.BlockSpec(block_shape, index_map, pipeline_mode=pl.Buffered(1))


def _layer_matrix(layer, rows, cols):
    return _resident((None, rows, cols), lambda *_: (layer, 0, 0))


def _matrix(rows, cols):
    return _resident((rows, cols), lambda *_: (0, 0))


def _cast_specs(layer, rows, cols, n_steps, step):
    assert rows % n_steps == 0
    rb = rows // n_steps
    return (pl.BlockSpec((None, rb, cols), lambda *ids: (layer, step(*ids), 0)),
            pl.BlockSpec((rb, cols), lambda *ids: (step(*ids), 0)))


def _full(shape):
    return pl.BlockSpec(shape, lambda *_: (0,) * len(shape))


def _drop_first_ref(body):
    def wrapped(_, *refs):
        return body(*refs)
    return wrapped


def _call_carrying(body, carried, carried_out, *, in_specs, args, **kwargs):
    if carried is None:
        return pl.pallas_call(body, in_specs=in_specs, **kwargs)(*args)
    return pl.pallas_call(
        _drop_first_ref(body), in_specs=[pl.BlockSpec(memory_space=pl.ANY)] + in_specs,
        input_output_aliases={0: carried_out}, **kwargs)(carried, *args)


def _rms(x, g):
    return x * lax.rsqrt(jnp.mean(x * x, axis=-1, keepdims=True) + EPS) * g


def _modulated_rms(x, g, scale, shift):
    return x * lax.rsqrt(jnp.mean(x * x, axis=-1, keepdims=True) + EPS) * (g * (1.0 + scale)) + shift


def _center_scale(x):
    mu = jnp.mean(x, axis=-1, keepdims=True)
    xc = x - mu
    var = jnp.mean(xc * xc, axis=-1, keepdims=True)
    return xc * lax.rsqrt(var + EPS)


def _dot(a, b):
    return jnp.dot(a, b, preferred_element_type=F32)


def _col(i):
    return slice(i * D_MODEL, (i + 1) * D_MODEL)


def _head(hd):
    return slice(hd * HEAD_DIM, (hd + 1) * HEAD_DIM)


def _ada_kernel(cp_ref, cs_ref, w_ref, b_ref, op_ref, os_ref):
    n_prompt = cp_ref.shape[0]
    c = jnp.concatenate([cp_ref[...], cs_ref[...]], axis=0).astype(BF16)
    mod = _dot(c, w_ref[...].astype(BF16)) + b_ref[...]
    op_ref[...] = mod[:n_prompt]
    os_ref[...] = mod[n_prompt:]


def _ada_call(c_prompt, c_sample, w_ada, b_ada):
    depth, d, n = w_ada.shape
    bp, bs = c_prompt.shape[0], c_sample.shape[0]
    return pl.pallas_call(
        _ada_kernel,
        out_shape=(jax.ShapeDtypeStruct((depth, bp, n), F32),
                   jax.ShapeDtypeStruct((depth, bs, n), F32)),
        grid=(depth, n // ADA_BLOCK_N),
        in_specs=[
            _full((bp, d)),
            _full((bs, d)),
            pl.BlockSpec((None, d, ADA_BLOCK_N), lambda l, j: (l, 0, j)),
            pl.BlockSpec((None, 1, ADA_BLOCK_N), lambda l, j: (l, 0, j)),
        ],
        out_specs=(pl.BlockSpec((None, bp, ADA_BLOCK_N), lambda l, j: (l, 0, j)),
                   pl.BlockSpec((None, bs, ADA_BLOCK_N), lambda l, j: (l, 0, j))),
        compiler_params=pltpu.CompilerParams(
            dimension_semantics=("arbitrary", "arbitrary"), vmem_limit_bytes=VMEM_LIMIT_BYTES),
        name="ada",
    )(c_prompt, c_sample, w_ada, b_ada.reshape(depth, 1, n))


def _retention_one_token(c_ref, q_ref, k_ref, v_ref, s_ref, o_ref, sn_ref, row):
    n_rows = q_ref.shape[0]

    def rows_of(ref, hd):
        return jnp.concatenate([ref[a, pl.ds(row, 1), _head(hd)] for a in range(n_rows)], axis=0)

    for hd in range(HEADS):
        q = rows_of(q_ref, hd)
        k = rows_of(k_ref, hd)
        v = rows_of(v_ref, hd)
        decay, xi, zeta, g_l = c_ref[0, hd], c_ref[1, hd], c_ref[2, hd], c_ref[3, hd]
        sc = jnp.sum(q * k, axis=-1, keepdims=True) * decay
        pad = jnp.zeros((CHUNK - n_rows, HEAD_DIM), F32)
        q_t = jnp.concatenate([q, pad], axis=0).T
        kz_t = jnp.concatenate([k * zeta, pad], axis=0).T
        qs_rows = []
        for b in range(n_rows):
            state = s_ref[b, hd]
            qs_rows.append(jnp.sum(q_t[:, b:b + 1] * state, axis=0, keepdims=True))
            sn_ref[b, hd] = state * g_l + kz_t[:, b:b + 1] * v[b:b + 1, :]
        o = sc * v + jnp.concatenate(qs_rows, axis=0) * xi
        for a in range(n_rows):
            o_ref[a, pl.ds(row, 1), _head(hd)] = o[a:a + 1, :]


def _ffn_body(x, sh2, sc2, gt2, n2g, w1_ref, w2_ref, h_s):
    h_s[...] = _modulated_rms(x, n2g, sc2, sh2).astype(BF16)
    acc = None
    for j in range(D_FF // FFN_SLAB):
        sl = slice(j * FFN_SLAB, (j + 1) * FFN_SLAB)
        hid = _dot(h_s[...], w1_ref[:, sl])
        hid = jnp.square(jnp.maximum(hid, 0.0)).astype(BF16)
        part = _dot(hid, w2_ref[sl, :])
        acc = part if acc is None else acc + part
    return x + gt2 * acc


def _ffn_kernel(*refs, layer, final):
    (c_ref, x_ref, mod_ref, n2g_ref, fg_ref, w1_ref, w2_ref, q_ref, k_ref, v_ref, s_ref) = refs[:11]
    h_s = refs[-1]
    if final:
        y_ref, o_ref, sn_ref = refs[11:14]
    else:
        win_ref, wout_ref, y_ref, o_ref, sn_ref, winb_ref, woutb_ref = refs[11:18]
        winb_ref[...] = win_ref[...].astype(BF16)
        woutb_ref[...] = wout_ref[...].astype(BF16)
    step = pl.program_id(0) * pl.num_programs(1) + pl.program_id(1)
    _retention_one_token(c_ref, q_ref, k_ref, v_ref, s_ref, o_ref, sn_ref, step % SUBLANES)
    mod = mod_ref.at[pl.ds(pl.program_id(0), 1)]
    y = _ffn_body(x_ref[...], mod[:, _col(3)], mod[:, _col(4)], mod[:, _col(5)],
                  n2g_ref[layer:layer + 1, :], w1_ref, w2_ref, h_s)
    if final:
        y = _rms(y, fg_ref[...])
    y_ref[...] = y


def _ffn_call(layer, x, mod, n2g, fg, w1, w2, consts1, q, k, v, state_all, new_state_all, w_in, w_out,
              *, final):
    b, t, d = x.shape
    depth = mod.shape[0]
    tm = TILE_M
    nj = t // tm
    nb, n_steps, _ = q.shape
    assert n_steps == b * nj and n_steps % SUBLANES == 0
    step = lambda i, j: i * nj + j
    rows_spec = pl.BlockSpec((nb, SUBLANES, d), lambda i, j: (0, step(i, j) // SUBLANES, 0))
    state_spec = pl.BlockSpec((None, nb, None, HEADS, HEAD_DIM, HEAD_DIM),
                              lambda i, j: (layer, 0, step(i, j), 0, 0, 0))
    in_specs = [
        pl.BlockSpec(memory_space=pltpu.SMEM),
        pl.BlockSpec((None, tm, d), lambda i, j: (i, j, 0)),
        _layer_matrix(layer, b, 6 * d),
        _resident((depth, d), lambda i, j: (0, 0)),
        _full((1, d)),
        _matrix(d, D_FF),
        _matrix(D_FF, d),
        rows_spec, rows_spec, rows_spec,
        state_spec,
    ]
    args = [consts1, x, mod, n2g, fg, w1, w2, q, k, v, state_all]
    out_shape = [jax.ShapeDtypeStruct(x.shape, F32),
                 jax.ShapeDtypeStruct(q.shape, F32),
                 jax.ShapeDtypeStruct(state_all.shape, F32)]
    out_specs = [pl.BlockSpec((None, tm, d), lambda i, j: (i, j, 0)), rows_spec, state_spec]
    if not final:
        for w in (w_in, w_out):
            src, dst = _cast_specs(layer + 1, w.shape[1], w.shape[2], n_steps, step)
            in_specs.append(src)
            args.append(w)
            out_specs.append(dst)
            out_shape.append(jax.ShapeDtypeStruct(w.shape[1:], BF16))
    return _call_carrying(
        functools.partial(_ffn_kernel, layer=layer, final=final), new_state_all, 2,
        out_shape=tuple(out_shape),
        grid=(b, nj),
        in_specs=in_specs,
        args=args,
        out_specs=tuple(out_specs),
        scratch_shapes=[pltpu.VMEM((tm, d), BF16)],
        compiler_params=pltpu.CompilerParams(
            dimension_semantics=("arbitrary", "arbitrary"), vmem_limit_bytes=VMEM_LIMIT_BYTES),
        name="ffn",
    )


def _mix_kernel(gl_ref, x_ref, mod_ref, cos_ref, sin_ref, n1g_ref, lng_ref, lnb_ref, gng_ref,
                ws_ref, bst_ref, decay_ref, xi_ref, zeta_ref, win_ref, wout_ref, wf1_ref, wf2_ref,
                xo_ref, s_ref, wf1b_ref, wf2b_ref,
                h_s, p_s, van_s, q_s, k_s, kz_s, v_s, a_s, on_s, *, layer):
    n_chunks = x_ref.shape[0] // CHUNK
    this = slice(layer, layer + 1)

    @pl.when(pl.program_id(1) == 0)
    def _():
        s_ref[...] = jnp.zeros_like(s_ref)

    wf1b_ref[...] = wf1_ref[...].astype(BF16)
    wf2b_ref[...] = wf2_ref[...].astype(BF16)

    mod = mod_ref.at[pl.ds(pl.program_id(0), 1)]
    sh1 = mod[:, _col(0)]
    sc1 = mod[:, _col(1)]
    gt1 = mod[:, _col(2)]
    h_s[...] = _modulated_rms(x_ref[...], n1g_ref[this, :], sc1, sh1).astype(BF16)

    def proj(slab):
        return _dot(h_s[...], win_ref[:, _col(slab)])

    def rows(c):
        return slice(c * CHUNK, (c + 1) * CHUNK)

    van_s[...] = (_center_scale(proj(SLAB_VA)) * lng_ref[this, :] + lnb_ref[this, :]).astype(BF16)
    p_s[...] = proj(SLAB_U)
    tril = (lax.broadcasted_iota(jnp.int32, (CHUNK, CHUNK), 0)
            >= lax.broadcasted_iota(jnp.int32, (CHUNK, CHUNK), 1))
    for g in range(GROUPS):
        w_g = jnp.where(tril, ws_ref[g], 0.0).astype(BF16)
        gc = slice(g * GROUP_DIM, (g + 1) * GROUP_DIM)
        for c in range(n_chunks):
            z = _dot(w_g, van_s[rows(c), gc]) + bst_ref[:, g:g + 1]
            a_s[rows(c), gc] = p_s[rows(c), gc] * z
    a_s[...] = jax.nn.sigmoid(proj(SLAB_GA)) * a_s[...]

    half = HEAD_DIM // 2

    def rotate(c, hd, scale=None):
        lo = slice(hd * HEAD_DIM, hd * HEAD_DIM + half)
        hi = slice(hd * HEAD_DIM + half, (hd + 1) * HEAD_DIM)
        x1 = p_s[rows(c), lo]
        x2 = p_s[rows(c), hi]
        cos = cos_ref[rows(c), :]
        sin = sin_ref[rows(c), :]
        if scale is not None:
            cos = cos * scale
            sin = sin * scale
        return lo, hi, x1 * cos - x2 * sin, x1 * sin + x2 * cos

    p_s[...] = proj(SLAB_Q)
    for c in range(n_chunks):
        for hd in range(HEADS):
            lo, hi, r1, r2 = rotate(c, hd)
            q_s[rows(c), lo] = r1.astype(BF16)
            q_s[rows(c), hi] = r2.astype(BF16)
    p_s[...] = proj(SLAB_K)
    k_scale = HEAD_DIM ** -0.5
    for c in range(n_chunks):
        for hd in range(HEADS):
            lo, hi, r1, r2 = rotate(c, hd, k_scale)
            k_s[rows(c), lo] = r1.astype(BF16)
            k_s[rows(c), hi] = r2.astype(BF16)
            zr = slice(c * CHUNK % RET_CHUNK, c * CHUNK % RET_CHUNK + CHUNK)
            kz_s[rows(c), lo] = (r1 * zeta_ref[zr, lo]).astype(BF16)
            kz_s[rows(c), hi] = (r2 * zeta_ref[zr, hi]).astype(BF16)
    v_s[...] = proj(SLAB_VR).astype(BF16)

    for c in range(x_ref.shape[0] // RET_CHUNK):
        rc = slice(c * RET_CHUNK, (c + 1) * RET_CHUNK)
        for hd in range(HEADS):
            hc = _head(hd)
            qc = q_s[rc, hc]
            vc = v_s[rc, hc]
            state = s_ref[hd]
            sc = lax.dot_general(qc, k_s[rc, hc], (((1,), (1,)), ((), ())),
                                 preferred_element_type=F32) * decay_ref[hd]
            o = _dot(sc.astype(BF16), vc) + _dot(qc, state.astype(BF16)) * xi_ref[:, hc]
            s_ref[hd] = state * gl_ref[hd] + lax.dot_general(
                kz_s[rc, hc], vc, (((0,), (0,)), ((), ())), preferred_element_type=F32)
            on_s[rc, hc] = _center_scale(o) * gng_ref[this, hc]

    on_s[...] = jax.nn.silu(proj(SLAB_G)) * on_s[...]
    h_s[...] = (a_s[...] + jax.nn.sigmoid(proj(SLAB_GB)) * on_s[...]).astype(BF16)
    xo_ref[...] = x_ref[...] + gt1 * _dot(h_s[...], wout_ref[...])


def _mix_call(layer, x, mod, cos, sin, n1g, lng, lnb, gng, ws, bst, decay, xi, zeta, gl, w_in, w_out,
              w_ff1, w_ff2, ret_all):
    b, t, d = x.shape
    depth = w_ff1.shape[0]
    tm = TILE_M
    nj = t // tm
    step = lambda i, j: i * nj + j
    ff1_src, ff1_dst = _cast_specs(layer, d, D_FF, b * nj, step)
    ff2_src, ff2_dst = _cast_specs(layer, D_FF, d, b * nj, step)
    params = _resident((depth, d), lambda i, j: (0, 0))
    return _call_carrying(
        functools.partial(_mix_kernel, layer=layer), ret_all, 1,
        out_shape=(jax.ShapeDtypeStruct(x.shape, F32),
                   jax.ShapeDtypeStruct((depth, b, HEADS, HEAD_DIM, HEAD_DIM), F32),
                   jax.ShapeDtypeStruct((d, D_FF), BF16),
                   jax.ShapeDtypeStruct((D_FF, d), BF16)),
        grid=(b, nj),
        in_specs=[
            pl.BlockSpec(memory_space=pltpu.SMEM),
            pl.BlockSpec((None, tm, d), lambda i, j: (i, j, 0)),
            _layer_matrix(layer, b, 6 * d),
            pl.BlockSpec((tm, HEAD_DIM // 2), lambda i, j: (j, 0)),
            pl.BlockSpec((tm, HEAD_DIM // 2), lambda i, j: (j, 0)),
            params, params, params, params,
            _resident((None, GROUPS, CHUNK, CHUNK), lambda i, j: (layer, 0, 0, 0)),
            _layer_matrix(layer, CHUNK, GROUPS),
            _resident((HEADS, RET_CHUNK, RET_CHUNK), lambda i, j: (0, 0, 0)),
            _resident((RET_CHUNK, d), lambda i, j: (0, 0)),
            _resident((RET_CHUNK, d), lambda i, j: (0, 0)),
            _matrix(d, N_SLABS * d),
            _matrix(d, d),
            ff1_src, ff2_src,
        ],
        args=[gl, x, mod, cos, sin, n1g, lng, lnb, gng, ws, bst, decay, xi, zeta, w_in, w_out,
              w_ff1, w_ff2],
        out_specs=(
            pl.BlockSpec((None, tm, d), lambda i, j: (i, j, 0)),
            pl.BlockSpec((None, None, HEADS, HEAD_DIM, HEAD_DIM), lambda i, j: (layer, i, 0, 0, 0)),
            ff1_dst, ff2_dst,
        ),
        scratch_shapes=[
            pltpu.VMEM((tm, d), BF16),
            pltpu.VMEM((tm, d), F32),
            pltpu.VMEM((tm, d), BF16),
            pltpu.VMEM((tm, d), BF16),
            pltpu.VMEM((tm, d), BF16),
            pltpu.VMEM((tm, d), BF16),
            pltpu.VMEM((tm, d), BF16),
            pltpu.VMEM((tm, d), F32),
            pltpu.VMEM((tm, d), F32),
        ],
        compiler_params=pltpu.CompilerParams(
            dimension_semantics=("arbitrary", "arbitrary"), vmem_limit_bytes=VMEM_LIMIT_BYTES),
        name="mix",
    )


def _sproj_kernel(x_ref, mod_ref, cos_ref, sin_ref, n1g_ref, lng_ref, lnb_ref, w0_ref, b0_ref,
                  win_ref, van_ref, ma_ref, q_ref, k_ref, v_ref, g_ref, gb_ref, h_s, *, layer):
    this = slice(layer, layer + 1)
    sh1 = mod_ref[:, _col(0)]
    sc1 = mod_ref[:, _col(1)]
    h_s[...] = _modulated_rms(x_ref[...], n1g_ref[this, :], sc1, sh1).astype(BF16)

    def proj(slab):
        return _dot(h_s[...], win_ref[:, _col(slab)])

    van = _center_scale(proj(SLAB_VA)) * lng_ref[this, :] + lnb_ref[this, :]
    van_ref[...] = van
    z = van * w0_ref[this, :] + b0_ref[this, :]
    ma_ref[...] = jax.nn.sigmoid(proj(SLAB_GA)) * (proj(SLAB_U) * z)

    half = HEAD_DIM // 2
    cos = cos_ref[...]
    sin = sin_ref[...]

    def rotary_to(dst_ref, p, scale):
        for hd in range(HEADS):
            lo = slice(hd * HEAD_DIM, hd * HEAD_DIM + half)
            hi = slice(hd * HEAD_DIM + half, (hd + 1) * HEAD_DIM)
            x1 = p[:, lo]
            x2 = p[:, hi]
            dst_ref[:, lo] = (x1 * cos - x2 * sin) * scale
            dst_ref[:, hi] = (x1 * sin + x2 * cos) * scale

    rotary_to(q_ref, proj(SLAB_Q), 1.0)
    rotary_to(k_ref, proj(SLAB_K), HEAD_DIM ** -0.5)
    v_ref[...] = proj(SLAB_VR)
    g_ref[...] = proj(SLAB_G)
    gb_ref[...] = proj(SLAB_GB)


def _sproj_call(layer, x, mod, cos, sin, n1g, lng, lnb, w0, b0, w_in, van_all):
    n, d = x.shape
    depth = mod.shape[0]
    out = jax.ShapeDtypeStruct((n, d), F32)
    params = _full((depth, d))
    return _call_carrying(
        functools.partial(_sproj_kernel, layer=layer), van_all, 0,
        out_shape=(jax.ShapeDtypeStruct((depth, n, d), F32),) + (out,) * 6,
        grid=(1,),
        in_specs=[
            _full((n, d)),
            pl.BlockSpec((None, n, 6 * d), lambda i: (layer, 0, 0)),
            _full((1, HEAD_DIM // 2)), _full((1, HEAD_DIM // 2)),
            params, params, params, params, params,
            _matrix(d, N_SLABS * d),
        ],
        args=[x, mod, cos, sin, n1g, lng, lnb, w0, b0, w_in],
        out_specs=(pl.BlockSpec((None, n, d), lambda i: (layer, 0, 0)),) + (_full((n, d)),) * 6,
        scratch_shapes=[pltpu.VMEM((n, d), BF16)],
        compiler_params=pltpu.CompilerParams(
            dimension_semantics=("arbitrary",), vmem_limit_bytes=VMEM_LIMIT_BYTES),
        name="sproj",
    )


def _spost_kernel(o_ref, g_ref, gb_ref, ma_ref, x_ref, mod_ref, gng_ref, n2g_ref, fg_ref,
                  wout_ref, w1_ref, w2_ref, y_ref, h_s, *, layer, final):
    this = slice(layer, layer + 1)
    for hd in range(HEADS):
        hc = _head(hd)
        r = jax.nn.silu(g_ref[:, hc]) * (_center_scale(o_ref[:, hc]) * gng_ref[this, hc])
        h_s[:, hc] = (ma_ref[:, hc] + jax.nn.sigmoid(gb_ref[:, hc]) * r).astype(BF16)
    x1 = x_ref[...] + mod_ref[:, _col(2)] * _dot(h_s[...], wout_ref[...])
    y = _ffn_body(x1, mod_ref[:, _col(3)], mod_ref[:, _col(4)], mod_ref[:, _col(5)],
                  n2g_ref[this, :], w1_ref, w2_ref, h_s)
    if final:
        y = _rms(y, fg_ref[...])
    y_ref[...] = y


def _spost_call(layer, o, g, gb, ma, x, mod, gng, n2g, fg, w_out, w1, w2, *, final):
    n, d = x.shape
    depth = mod.shape[0]
    return pl.pallas_call(
        functools.partial(_spost_kernel, layer=layer, final=final),
        out_shape=jax.ShapeDtypeStruct((n, d), F32),
        grid=(1,),
        in_specs=[
            _full((n, d)), _full((n, d)), _full((n, d)), _full((n, d)), _full((n, d)),
            pl.BlockSpec((None, n, 6 * d), lambda i: (layer, 0, 0)),
            _full((depth, d)), _full((depth, d)),
            _full((1, d)),
            _matrix(d, d),
            _matrix(d, D_FF),
            _matrix(D_FF, d),
        ],
        out_specs=_full((n, d)),
        scratch_shapes=[pltpu.VMEM((n, d), BF16)],
        compiler_params=pltpu.CompilerParams(
            dimension_semantics=("arbitrary",), vmem_limit_bytes=VMEM_LIMIT_BYTES),
        name="spost",
    )(o, g, gb, ma, x, mod, gng, n2g, fg, w_out, w1, w2)


def _rope_tables(pos0, t):
    half = HEAD_DIM // 2
    inv = ROPE_BASE ** (-jnp.arange(half, dtype=F32) / half)
    pos = pos0 + jnp.arange(t, dtype=F32)
    ang = pos[:, None] * inv[None, :]
    return jnp.cos(ang), jnp.sin(ang)


def _retention_tables(length):
    log_g = jnp.log1p(-jnp.exp2(-5.0 - jnp.arange(HEADS, dtype=F32)))
    idx = jnp.arange(length, dtype=F32)
    diff = idx[:, None] - idx[None, :]
    decay = jnp.where(diff[None] >= 0.0,
                      jnp.exp(jnp.maximum(diff, 0.0)[None] * log_g[:, None, None]), 0.0)
    xi = jnp.exp((idx[:, None] + 1.0) * log_g[None, :])
    zeta = jnp.exp((length - 1.0 - idx)[:, None] * log_g[None, :])
    g_len = jnp.exp(length * log_g)
    return decay, xi, zeta, g_len


def kernel(x_prompt, x_sample, state_ret, c_prompt, c_sample, w_ada, b_ada, norm1_g, w_in, ln_v_g,
           ln_v_b, w_s, b_s, gn_g, w_out, norm2_g, w_ff1, w_ff2, final_g):
    depth = w_in.shape[0]
    batch, seq, d = x_prompt.shape
    n_sample = x_sample.shape[0]
    n_steps = batch * (seq // TILE_M)
    assert d == D_MODEL and seq % TILE_M == 0 and x_sample.shape[1] == 1
    assert TILE_M % RET_CHUNK == 0 and RET_CHUNK % CHUNK == 0
    assert n_sample % n_steps == 0

    w_in_b = w_in[0].astype(BF16)
    w_out_b = w_out[0].astype(BF16)
    n1g, lng, lnb, gng, n2g = norm1_g, ln_v_g, ln_v_b, gn_g, norm2_g
    fg = final_g.reshape(1, d)

    mod_p, mod_s = _ada_call(c_prompt, c_sample, w_ada, b_ada)

    cos_p, sin_p = _rope_tables(0.0, seq)
    cos_s, sin_s = _rope_tables(float(PAST_LEN), 1)
    decay, xi, zeta, g_len = _retention_tables(RET_CHUNK)
    xi_cols = jnp.repeat(xi, HEAD_DIM, axis=1)
    zeta_cols = jnp.repeat(zeta, HEAD_DIM, axis=1)
    decay1, xi1, zeta1, g_len1 = _retention_tables(1)
    consts1 = jnp.stack([decay1[:, 0, 0], xi1[0], zeta1[0], g_len1])
    b_s_t = jnp.swapaxes(b_s, 1, 2)
    w0_cols = jnp.repeat(w_s[:, :, 0, 0], GROUP_DIM, axis=1)
    b0_cols = jnp.repeat(b_s[:, :, 0], GROUP_DIM, axis=1)

    xp = x_prompt
    xs = x_sample.reshape(n_sample, d)
    ret_prompt = new_state = van_all = None
    per_step = n_sample // n_steps
    split = lambda a: a.reshape(per_step, n_steps, d)
    state_in = state_ret.reshape(depth, per_step, n_steps, *state_ret.shape[2:])
    for l in range(depth):
        final = l == depth - 1
        xp, ret_prompt, w_ff1_b, w_ff2_b = _mix_call(
            l, xp, mod_p, cos_p, sin_p, n1g, lng, lnb, gng, w_s, b_s_t, decay, xi_cols, zeta_cols,
            g_len, w_in_b, w_out_b, w_ff1, w_ff2, ret_prompt)
        van_all, ma, q, k, v, g, gb = _sproj_call(l, xs, mod_s, cos_s, sin_s, n1g, lng, lnb, w0_cols,
                                                  b0_cols, w_in_b, van_all)
        xp, o, new_state, *next_w = _ffn_call(
            l, xp, mod_p, n2g, fg, w_ff1_b, w_ff2_b, consts1, split(q), split(k), split(v),
            state_in, new_state, w_in, w_out, final=final)
        xs = _spost_call(l, o.reshape(n_sample, d), g, gb, ma, xs, mod_s, gng, n2g, fg, w_out_b,
                         w_ff1_b, w_ff2_b, final=final)
        if not final:
            w_in_b, w_out_b = next_w

    return (xp, xs.reshape(x_sample.shape), ret_prompt, new_state.reshape(state_ret.shape),
            van_all.reshape(depth, n_sample, 1, d))
```

```python
import functools

import jax
import jax.numpy as jnp
from jax import lax
from jax.experimental import pallas as pl
from jax.experimental.pallas import tpu as pltpu

F32 = jnp.float32
BF16 = jnp.bfloat16

D_MODEL = 1024
CHUNK = 128
GROUPS = 4
GROUP_DIM = D_MODEL // GROUPS
HEADS = 4
HEAD_DIM = D_MODEL // HEADS
D_FF = 4 * D_MODEL
N_SLABS = 8
SLAB_U, SLAB_VA, SLAB_Q, SLAB_K, SLAB_VR, SLAB_G, SLAB_GA, SLAB_GB = range(N_SLABS)
PAST_LEN = 16384
ROPE_BASE = 10000.0
EPS = 1e-6
SUBLANES = 8

RET_CHUNK = 256
TILE_M = 512
FFN_SLAB = 4096
ADA_BLOCK_K = 256
VMEM_LIMIT_BYTES = 56 * 1024 * 1024


def _resident(block_shape, index_map):
    return pl.BlockSpec(block_shape, index_map, pipeline_mode=pl.Buffered(1))


def _layer_matrix(layer, rows, cols):
    return _resident((None, rows, cols), lambda *_: (layer, 0, 0))


def _matrix(rows, cols):
    return _resident((rows, cols), lambda *_: (0, 0))


def _cast_specs(layer, rows, cols, n_steps, step):
    assert rows % n_steps == 0
    rb = rows // n_steps
    return (pl.BlockSpec((None, rb, cols), lambda *ids: (layer, step(*ids), 0)),
            pl.BlockSpec((rb, cols), lambda *ids: (step(*ids), 0)))


def _full(shape):
    return pl.BlockSpec(shape, lambda *_: (0,) * len(shape))


def _drop_first_ref(body):
    def wrapped(_, *refs):
        return body(*refs)
    return wrapped


def _call_carrying(body, carried, carried_out, *, in_specs, args, **kwargs):
    if carried is None:
        return pl.pallas_call(body, in_specs=in_specs, **kwargs)(*args)
    return pl.pallas_call(
        _drop_first_ref(body), in_specs=[pl.BlockSpec(memory_space=pl.ANY)] + in_specs,
        input_output_aliases={0: carried_out}, **kwargs)(carried, *args)


def _rms(x, g):
    return x * lax.rsqrt(jnp.mean(x * x, axis=-1, keepdims=True) + EPS) * g


def _modulated_rms(x, g, scale, shift):
    return x * lax.rsqrt(jnp.mean(x * x, axis=-1, keepdims=True) + EPS) * (g * (1.0 + scale)) + shift


def _center_scale(x):
    mu = jnp.mean(x, axis=-1, keepdims=True)
    xc = x - mu
    var = jnp.mean(xc * xc, axis=-1, keepdims=True)
    return xc * lax.rsqrt(var + EPS)


def _dot(a, b):
    return jnp.dot(a, b, preferred_element_type=F32)


def _col(i):
    return slice(i * D_MODEL, (i + 1) * D_MODEL)


def _head(hd):
    return slice(hd * HEAD_DIM, (hd + 1) * HEAD_DIM)


def _ada_kernel(cp_ref, cs_ref, w_ref, b_ref, op_ref, os_ref):
    @pl.when(pl.program_id(1) == 0)
    def _():
        op_ref[...] = jnp.broadcast_to(b_ref[...], op_ref.shape)
        os_ref[...] = jnp.broadcast_to(b_ref[...], os_ref.shape)

    n_prompt = cp_ref.shape[0]
    c = jnp.concatenate([cp_ref[...], cs_ref[...]], axis=0).astype(BF16)
    part = _dot(c, w_ref[...].astype(BF16))
    op_ref[...] += part[:n_prompt]
    os_ref[...] += part[n_prompt:]


def _ada_call(c_prompt, c_sample, w_ada, b_ada):
    depth, d, n = w_ada.shape
    bp, bs = c_prompt.shape[0], c_sample.shape[0]
    return pl.pallas_call(
        _ada_kernel,
        out_shape=(jax.ShapeDtypeStruct((depth, bp, n), F32),
                   jax.ShapeDtypeStruct((depth, bs, n), F32)),
        grid=(depth, d // ADA_BLOCK_K),
        in_specs=[
            pl.BlockSpec((bp, ADA_BLOCK_K), lambda l, k: (0, k)),
            pl.BlockSpec((bs, ADA_BLOCK_K), lambda l, k: (0, k)),
            pl.BlockSpec((None, ADA_BLOCK_K, n), lambda l, k: (l, k, 0)),
            pl.BlockSpec((None, 1, n), lambda l, k: (l, 0, 0)),
        ],
        out_specs=(pl.BlockSpec((None, bp, n), lambda l, k: (l, 0, 0)),
                   pl.BlockSpec((None, bs, n), lambda l, k: (l, 0, 0))),
        compiler_params=pltpu.CompilerParams(
            dimension_semantics=("arbitrary", "arbitrary"), vmem_limit_bytes=VMEM_LIMIT_BYTES),
        name="ada",
    )(c_prompt, c_sample, w_ada, b_ada.reshape(depth, 1, n))


def _retention_one_token(c_ref, q_ref, k_ref, v_ref, s_ref, o_ref, sn_ref, row):
    n_rows = q_ref.shape[0]

    def rows_of(ref, hd):
        return jnp.concatenate([ref[a, pl.ds(row, 1), _head(hd)] for a in range(n_rows)], axis=0)

    for hd in range(HEADS):
        q = rows_of(q_ref, hd)
        k = rows_of(k_ref, hd)
        v = rows_of(v_ref, hd)
        decay, xi, zeta, g_l = c_ref[0, hd], c_ref[1, hd], c_ref[2, hd], c_ref[3, hd]
        sc = jnp.sum(q * k, axis=-1, keepdims=True) * decay
        pad = jnp.zeros((CHUNK - n_rows, HEAD_DIM), F32)
        q_t = jnp.concatenate([q, pad], axis=0).T
        kz_t = jnp.concatenate([k * zeta, pad], axis=0).T
        qs_rows = []
        for b in range(n_rows):
            state = s_ref[b, hd]
            qs_rows.append(jnp.sum(q_t[:, b:b + 1] * state, axis=0, keepdims=True))
            sn_ref[b, hd] = state * g_l + kz_t[:, b:b + 1] * v[b:b + 1, :]
        o = sc * v + jnp.concatenate(qs_rows, axis=0) * xi
        for a in range(n_rows):
            o_ref[a, pl.ds(row, 1), _head(hd)] = o[a:a + 1, :]


def _ffn_body(x, sh2, sc2, gt2, n2g, w1_ref, w2_ref, h_s):
    h_s[...] = _modulated_rms(x, n2g, sc2, sh2).astype(BF16)
    acc = None
    for j in range(D_FF // FFN_SLAB):
        sl = slice(j * FFN_SLAB, (j + 1) * FFN_SLAB)
        hid = _dot(h_s[...], w1_ref[:, sl])
        hid = jnp.square(jnp.maximum(hid, 0.0)).astype(BF16)
        part = _dot(hid, w2_ref[sl, :])
        acc = part if acc is None else acc + part
    return x + gt2 * acc


def _ffn_kernel(*refs, layer, final):
    (c_ref, x_ref, mod_ref, n2g_ref, fg_ref, w1_ref, w2_ref, q_ref, k_ref, v_ref, s_ref) = refs[:11]
    h_s = refs[-1]
    if final:
        y_ref, o_ref, sn_ref = refs[11:14]
    else:
        win_ref, wout_ref, y_ref, o_ref, sn_ref, winb_ref, woutb_ref = refs[11:18]
        winb_ref[...] = win_ref[...].astype(BF16)
        woutb_ref[...] = wout_ref[...].astype(BF16)
    step = pl.program_id(0) * pl.num_programs(1) + pl.program_id(1)
    _retention_one_token(c_ref, q_ref, k_ref, v_ref, s_ref, o_ref, sn_ref, step % SUBLANES)
    mod = mod_ref.at[pl.ds(pl.program_id(0), 1)]
    y = _ffn_body(x_ref[...], mod[:, _col(3)], mod[:, _col(4)], mod[:, _col(5)],
                  n2g_ref[layer:layer + 1, :], w1_ref, w2_ref, h_s)
    if final:
        y = _rms(y, fg_ref[...])
    y_ref[...] = y


def _ffn_call(layer, x, mod, n2g, fg, w1, w2, consts1, q, k, v, state_all, new_state_all, w_in, w_out,
              *, final):
    b, t, d = x.shape
    depth = mod.shape[0]
    tm = TILE_M
    nj = t // tm
    nb, n_steps, _ = q.shape
    assert n_steps == b * nj and n_steps % SUBLANES == 0
    step = lambda i, j: i * nj + j
    rows_spec = pl.BlockSpec((nb, SUBLANES, d), lambda i, j: (0, step(i, j) // SUBLANES, 0))
    state_spec = pl.BlockSpec((None, nb, None, HEADS, HEAD_DIM, HEAD_DIM),
                              lambda i, j: (layer, 0, step(i, j), 0, 0, 0))
    in_specs = [
        pl.BlockSpec(memory_space=pltpu.SMEM),
        pl.BlockSpec((None, tm, d), lambda i, j: (i, j, 0)),
        _layer_matrix(layer, b, 6 * d),
        _resident((depth, d), lambda i, j: (0, 0)),
        _full((1, d)),
        _matrix(d, D_FF),
        _matrix(D_FF, d),
        rows_spec, rows_spec, rows_spec,
        state_spec,
    ]
    args = [consts1, x, mod, n2g, fg, w1, w2, q, k, v, state_all]
    out_shape = [jax.ShapeDtypeStruct(x.shape, F32),
                 jax.ShapeDtypeStruct(q.shape, F32),
                 jax.ShapeDtypeStruct(state_all.shape, F32)]
    out_specs = [pl.BlockSpec((None, tm, d), lambda i, j: (i, j, 0)), rows_spec, state_spec]
    if not final:
        for w in (w_in, w_out):
            src, dst = _cast_specs(layer + 1, w.shape[1], w.shape[2], n_steps, step)
            in_specs.append(src)
            args.append(w)
            out_specs.append(dst)
            out_shape.append(jax.ShapeDtypeStruct(w.shape[1:], BF16))
    return _call_carrying(
        functools.partial(_ffn_kernel, layer=layer, final=final), new_state_all, 2,
        out_shape=tuple(out_shape),
        grid=(b, nj),
        in_specs=in_specs,
        args=args,
        out_specs=tuple(out_specs),
        scratch_shapes=[pltpu.VMEM((tm, d), BF16)],
        compiler_params=pltpu.CompilerParams(
            dimension_semantics=("arbitrary", "arbitrary"), vmem_limit_bytes=VMEM_LIMIT_BYTES),
        name="ffn",
    )


def _mix_kernel(gl_ref, x_ref, mod_ref, cos_ref, sin_ref, n1g_ref, lng_ref, lnb_ref, gng_ref,
                ws_ref, bst_ref, decay_ref, xi_ref, zeta_ref, win_ref, wout_ref, wf1_ref, wf2_ref,
                xo_ref, s_ref, wf1b_ref, wf2b_ref,
                h_s, p_s, van_s, q_s, k_s, kz_s, v_s, a_s, on_s, *, layer):
    n_chunks = x_ref.shape[0] // CHUNK
    this = slice(layer, layer + 1)

    @pl.when(pl.program_id(1) == 0)
    def _():
        s_ref[...] = jnp.zeros_like(s_ref)

    wf1b_ref[...] = wf1_ref[...].astype(BF16)
    wf2b_ref[...] = wf2_ref[...].astype(BF16)

    mod = mod_ref.at[pl.ds(pl.program_id(0), 1)]
    sh1 = mod[:, _col(0)]
    sc1 = mod[:, _col(1)]
    gt1 = mod[:, _col(2)]
    h_s[...] = _modulated_rms(x_ref[...], n1g_ref[this, :], sc1, sh1).astype(BF16)

    def proj(slab):
        return _dot(h_s[...], win_ref[:, _col(slab)])

    def rows(c):
        return slice(c * CHUNK, (c + 1) * CHUNK)

    van_s[...] = (_center_scale(proj(SLAB_VA)) * lng_ref[this, :] + lnb_ref[this, :]).astype(BF16)
    p_s[...] = proj(SLAB_U)
    tril = (lax.broadcasted_iota(jnp.int32, (CHUNK, CHUNK), 0)
            >= lax.broadcasted_iota(jnp.int32, (CHUNK, CHUNK), 1))
    for g in range(GROUPS):
        w_g = jnp.where(tril, ws_ref[g], 0.0).astype(BF16)
        gc = slice(g * GROUP_DIM, (g + 1) * GROUP_DIM)
        for c in range(n_chunks):
            z = _dot(w_g, van_s[rows(c), gc]) + bst_ref[:, g:g + 1]
            a_s[rows(c), gc] = p_s[rows(c), gc] * z
    a_s[...] = jax.nn.sigmoid(proj(SLAB_GA)) * a_s[...]

    half = HEAD_DIM // 2

    def rotate(c, hd, scale=None):
        lo = slice(hd * HEAD_DIM, hd * HEAD_DIM + half)
        hi = slice(hd * HEAD_DIM + half, (hd + 1) * HEAD_DIM)
        x1 = p_s[rows(c), lo]
        x2 = p_s[rows(c), hi]
        cos = cos_ref[rows(c), :]
        sin = sin_ref[rows(c), :]
        if scale is not None:
            cos = cos * scale
            sin = sin * scale
        return lo, hi, x1 * cos - x2 * sin, x1 * sin + x2 * cos

    p_s[...] = proj(SLAB_Q)
    for c in range(n_chunks):
        for hd in range(HEADS):
            lo, hi, r1, r2 = rotate(c, hd)
            q_s[rows(c), lo] = r1.astype(BF16)
            q_s[rows(c), hi] = r2.astype(BF16)
    p_s[...] = proj(SLAB_K)
    k_scale = HEAD_DIM ** -0.5
    for c in range(n_chunks):
        for hd in range(HEADS):
            lo, hi, r1, r2 = rotate(c, hd, k_scale)
            k_s[rows(c), lo] = r1.astype(BF16)
            k_s[rows(c), hi] = r2.astype(BF16)
            zr = slice(c * CHUNK % RET_CHUNK, c * CHUNK % RET_CHUNK + CHUNK)
            kz_s[rows(c), lo] = (r1 * zeta_ref[zr, lo]).astype(BF16)
            kz_s[rows(c), hi] = (r2 * zeta_ref[zr, hi]).astype(BF16)
    v_s[...] = proj(SLAB_VR).astype(BF16)

    for c in range(x_ref.shape[0] // RET_CHUNK):
        rc = slice(c * RET_CHUNK, (c + 1) * RET_CHUNK)
        for hd in range(HEADS):
            hc = _head(hd)
            qc = q_s[rc, hc]
            vc = v_s[rc, hc]
            state = s_ref[hd]
            sc = lax.dot_general(qc, k_s[rc, hc], (((1,), (1,)), ((), ())),
                                 preferred_element_type=F32) * decay_ref[hd]
            o = _dot(sc.astype(BF16), vc) + _dot(qc, state.astype(BF16)) * xi_ref[:, hc]
            s_ref[hd] = state * gl_ref[hd] + lax.dot_general(
                kz_s[rc, hc], vc, (((0,), (0,)), ((), ())), preferred_element_type=F32)
            on_s[rc, hc] = _center_scale(o) * gng_ref[this, hc]

    on_s[...] = jax.nn.silu(proj(SLAB_G)) * on_s[...]
    h_s[...] = (a_s[...] + jax.nn.sigmoid(proj(SLAB_GB)) * on_s[...]).astype(BF16)
    xo_ref[...] = x_ref[...] + gt1 * _dot(h_s[...], wout_ref[...])


def _mix_call(layer, x, mod, cos, sin, n1g, lng, lnb, gng, ws, bst, decay, xi, zeta, gl, w_in, w_out,
              w_ff1, w_ff2, ret_all):
    b, t, d = x.shape
    depth = w_ff1.shape[0]
    tm = TILE_M
    nj = t // tm
    step = lambda i, j: i * nj + j
    ff1_src, ff1_dst = _cast_specs(layer, d, D_FF, b * nj, step)
    ff2_src, ff2_dst = _cast_specs(layer, D_FF, d, b * nj, step)
    params = _resident((depth, d), lambda i, j: (0, 0))
    return _call_carrying(
        functools.partial(_mix_kernel, layer=layer), ret_all, 1,
        out_shape=(jax.ShapeDtypeStruct(x.shape, F32),
                   jax.ShapeDtypeStruct((depth, b, HEADS, HEAD_DIM, HEAD_DIM), F32),
                   jax.ShapeDtypeStruct((d, D_FF), BF16),
                   jax.ShapeDtypeStruct((D_FF, d), BF16)),
        grid=(b, nj),
        in_specs=[
            pl.BlockSpec(memory_space=pltpu.SMEM),
            pl.BlockSpec((None, tm, d), lambda i, j: (i, j, 0)),
            _layer_matrix(layer, b, 6 * d),
            pl.BlockSpec((tm, HEAD_DIM // 2), lambda i, j: (j, 0)),
            pl.BlockSpec((tm, HEAD_DIM // 2), lambda i, j: (j, 0)),
            params, params, params, params,
            _resident((None, GROUPS, CHUNK, CHUNK), lambda i, j: (layer, 0, 0, 0)),
            _layer_matrix(layer, CHUNK, GROUPS),
            _resident((HEADS, RET_CHUNK, RET_CHUNK), lambda i, j: (0, 0, 0)),
            _resident((RET_CHUNK, d), lambda i, j: (0, 0)),
            _resident((RET_CHUNK, d), lambda i, j: (0, 0)),
            _matrix(d, N_SLABS * d),
            _matrix(d, d),
            ff1_src, ff2_src,
        ],
        args=[gl, x, mod, cos, sin, n1g, lng, lnb, gng, ws, bst, decay, xi, zeta, w_in, w_out,
              w_ff1, w_ff2],
        out_specs=(
            pl.BlockSpec((None, tm, d), lambda i, j: (i, j, 0)),
            pl.BlockSpec((None, None, HEADS, HEAD_DIM, HEAD_DIM), lambda i, j: (layer, i, 0, 0, 0)),
            ff1_dst, ff2_dst,
        ),
        scratch_shapes=[
            pltpu.VMEM((tm, d), BF16),
            pltpu.VMEM((tm, d), F32),
            pltpu.VMEM((tm, d), BF16),
            pltpu.VMEM((tm, d), BF16),
            pltpu.VMEM((tm, d), BF16),
            pltpu.VMEM((tm, d), BF16),
            pltpu.VMEM((tm, d), BF16),
            pltpu.VMEM((tm, d), F32),
            pltpu.VMEM((tm, d), F32),
        ],
        compiler_params=pltpu.CompilerParams(
            dimension_semantics=("arbitrary", "arbitrary"), vmem_limit_bytes=VMEM_LIMIT_BYTES),
        name="mix",
    )


def _sproj_kernel(x_ref, mod_ref, cos_ref, sin_ref, n1g_ref, lng_ref, lnb_ref, w0_ref, b0_ref,
                  win_ref, van_ref, ma_ref, q_ref, k_ref, v_ref, g_ref, gb_ref, h_s, *, layer):
    this = slice(layer, layer + 1)
    sh1 = mod_ref[:, _col(0)]
    sc1 = mod_ref[:, _col(1)]
    h_s[...] = _modulated_rms(x_ref[...], n1g_ref[this, :], sc1, sh1).astype(BF16)

    def proj(slab):
        return _dot(h_s[...], win_ref[:, _col(slab)])

    van = _center_scale(proj(SLAB_VA)) * lng_ref[this, :] + lnb_ref[this, :]
    van_ref[...] = van
    z = van * w0_ref[this, :] + b0_ref[this, :]
    ma_ref[...] = jax.nn.sigmoid(proj(SLAB_GA)) * (proj(SLAB_U) * z)

    half = HEAD_DIM // 2
    cos = cos_ref[...]
    sin = sin_ref[...]

    def rotary_to(dst_ref, p, scale):
        for hd in range(HEADS):
            lo = slice(hd * HEAD_DIM, hd * HEAD_DIM + half)
            hi = slice(hd * HEAD_DIM + half, (hd + 1) * HEAD_DIM)
            x1 = p[:, lo]
            x2 = p[:, hi]
            dst_ref[:, lo] = (x1 * cos - x2 * sin) * scale
            dst_ref[:, hi] = (x1 * sin + x2 * cos) * scale

    rotary_to(q_ref, proj(SLAB_Q), 1.0)
    rotary_to(k_ref, proj(SLAB_K), HEAD_DIM ** -0.5)
    v_ref[...] = proj(SLAB_VR)
    g_ref[...] = proj(SLAB_G)
    gb_ref[...] = proj(SLAB_GB)


def _sproj_call(layer, x, mod, cos, sin, n1g, lng, lnb, w0, b0, w_in, van_all):
    n, d = x.shape
    depth = mod.shape[0]
    out = jax.ShapeDtypeStruct((n, d), F32)
    params = _full((depth, d))
    return _call_carrying(
        functools.partial(_sproj_kernel, layer=layer), van_all, 0,
        out_shape=(jax.ShapeDtypeStruct((depth, n, d), F32),) + (out,) * 6,
        grid=(1,),
        in_specs=[
            _full((n, d)),
            pl.BlockSpec((None, n, 6 * d), lambda i: (layer, 0, 0)),
            _full((1, HEAD_DIM // 2)), _full((1, HEAD_DIM // 2)),
            params, params, params, params, params,
            _matrix(d, N_SLABS * d),
        ],
        args=[x, mod, cos, sin, n1g, lng, lnb, w0, b0, w_in],
        out_specs=(pl.BlockSpec((None, n, d), lambda i: (layer, 0, 0)),) + (_full((n, d)),) * 6,
        scratch_shapes=[pltpu.VMEM((n, d), BF16)],
        compiler_params=pltpu.CompilerParams(
            dimension_semantics=("arbitrary",), vmem_limit_bytes=VMEM_LIMIT_BYTES),
        name="sproj",
    )


def _spost_kernel(o_ref, g_ref, gb_ref, ma_ref, x_ref, mod_ref, gng_ref, n2g_ref, fg_ref,
                  wout_ref, w1_ref, w2_ref, y_ref, h_s, *, layer, final):
    this = slice(layer, layer + 1)
    for hd in range(HEADS):
        hc = _head(hd)
        r = jax.nn.silu(g_ref[:, hc]) * (_center_scale(o_ref[:, hc]) * gng_ref[this, hc])
        h_s[:, hc] = (ma_ref[:, hc] + jax.nn.sigmoid(gb_ref[:, hc]) * r).astype(BF16)
    x1 = x_ref[...] + mod_ref[:, _col(2)] * _dot(h_s[...], wout_ref[...])
    y = _ffn_body(x1, mod_ref[:, _col(3)], mod_ref[:, _col(4)], mod_ref[:, _col(5)],
                  n2g_ref[this, :], w1_ref, w2_ref, h_s)
    if final:
        y = _rms(y, fg_ref[...])
    y_ref[...] = y


def _spost_call(layer, o, g, gb, ma, x, mod, gng, n2g, fg, w_out, w1, w2, *, final):
    n, d = x.shape
    depth = mod.shape[0]
    return pl.pallas_call(
        functools.partial(_spost_kernel, layer=layer, final=final),
        out_shape=jax.ShapeDtypeStruct((n, d), F32),
        grid=(1,),
        in_specs=[
            _full((n, d)), _full((n, d)), _full((n, d)), _full((n, d)), _full((n, d)),
            pl.BlockSpec((None, n, 6 * d), lambda i: (layer, 0, 0)),
            _full((depth, d)), _full((depth, d)),
            _full((1, d)),
            _matrix(d, d),
            _matrix(d, D_FF),
            _matrix(D_FF, d),
        ],
        out_specs=_full((n, d)),
        scratch_shapes=[pltpu.VMEM((n, d), BF16)],
        compiler_params=pltpu.CompilerParams(
            dimension_semantics=("arbitrary",), vmem_limit_bytes=VMEM_LIMIT_BYTES),
        name="spost",
    )(o, g, gb, ma, x, mod, gng, n2g, fg, w_out, w1, w2)


def _rope_tables(pos0, t):
    half = HEAD_DIM // 2
    inv = ROPE_BASE ** (-jnp.arange(half, dtype=F32) / half)
    pos = pos0 + jnp.arange(t, dtype=F32)
    ang = pos[:, None] * inv[None, :]
    return jnp.cos(ang), jnp.sin(ang)


def _retention_tables(length):
    log_g = jnp.log1p(-jnp.exp2(-5.0 - jnp.arange(HEADS, dtype=F32)))
    idx = jnp.arange(length, dtype=F32)
    diff = idx[:, None] - idx[None, :]
    decay = jnp.where(diff[None] >= 0.0,
                      jnp.exp(jnp.maximum(diff, 0.0)[None] * log_g[:, None, None]), 0.0)
    xi = jnp.exp((idx[:, None] + 1.0) * log_g[None, :])
    zeta = jnp.exp((length - 1.0 - idx)[:, None] * log_g[None, :])
    g_len = jnp.exp(length * log_g)
    return decay, xi, zeta, g_len


def kernel(x_prompt, x_sample, state_ret, c_prompt, c_sample, w_ada, b_ada, norm1_g, w_in, ln_v_g,
           ln_v_b, w_s, b_s, gn_g, w_out, norm2_g, w_ff1, w_ff2, final_g):
    depth = w_in.shape[0]
    batch, seq, d = x_prompt.shape
    n_sample = x_sample.shape[0]
    n_steps = batch * (seq // TILE_M)
    assert d == D_MODEL and seq % TILE_M == 0 and x_sample.shape[1] == 1
    assert TILE_M % RET_CHUNK == 0 and RET_CHUNK % CHUNK == 0
    assert n_sample % n_steps == 0

    w_in_b = w_in[0].astype(BF16)
    w_out_b = w_out[0].astype(BF16)
    n1g, lng, lnb, gng, n2g = norm1_g, ln_v_g, ln_v_b, gn_g, norm2_g
    fg = final_g.reshape(1, d)

    mod_p, mod_s = _ada_call(c_prompt, c_sample, w_ada, b_ada)

    cos_p, sin_p = _rope_tables(0.0, seq)
    cos_s, sin_s = _rope_tables(float(PAST_LEN), 1)
    decay, xi, zeta, g_len = _retention_tables(RET_CHUNK)
    xi_cols = jnp.repeat(xi, HEAD_DIM, axis=1)
    zeta_cols = jnp.repeat(zeta, HEAD_DIM, axis=1)
    decay1, xi1, zeta1, g_len1 = _retention_tables(1)
    consts1 = jnp.stack([decay1[:, 0, 0], xi1[0], zeta1[0], g_len1])
    b_s_t = jnp.swapaxes(b_s, 1, 2)
    w0_cols = jnp.repeat(w_s[:, :, 0, 0], GROUP_DIM, axis=1)
    b0_cols = jnp.repeat(b_s[:, :, 0], GROUP_DIM, axis=1)

    xp = x_prompt
    xs = x_sample.reshape(n_sample, d)
    ret_prompt = new_state = van_all = None
    per_step = n_sample // n_steps
    split = lambda a: a.reshape(per_step, n_steps, d)
    state_in = state_ret.reshape(depth, per_step, n_steps, *state_ret.shape[2:])
    for l in range(depth):
        final = l == depth - 1
        xp, ret_prompt, w_ff1_b, w_ff2_b = _mix_call(
            l, xp, mod_p, cos_p, sin_p, n1g, lng, lnb, gng, w_s, b_s_t, decay, xi_cols, zeta_cols,
            g_len, w_in_b, w_out_b, w_ff1, w_ff2, ret_prompt)
        van_all, ma, q, k, v, g, gb = _sproj_call(l, xs, mod_s, cos_s, sin_s, n1g, lng, lnb, w0_cols,
                                                  b0_cols, w_in_b, van_all)
        xp, o, new_state, *next_w = _ffn_call(
            l, xp, mod_p, n2g, fg, w_ff1_b, w_ff2_b, consts1, split(q), split(k), split(v),
            state_in, new_state, w_in, w_out, final=final)
        xs = _spost_call(l, o.reshape(n_sample, d), g, gb, ma, xs, mod_s, gng, n2g, fg, w_out_b,
                         w_ff1_b, w_ff2_b, final=final)
        if not final:
            w_in_b, w_out_b = next_w

    return (xp, xs.reshape(x_sample.shape), ret_prompt, new_state.reshape(state_ret.shape),
            van_all.reshape(depth, n_sample, 1, d))
```
